```python
import math
import jax, jax.numpy as jnp
from jax import lax
import numpy as np

D_MODEL = 2048
BATCH = 4
SEQ = 4096
DEPTH = 4

EPS = 1e-6
CHUNK = 128
A_GROUPS = 8
A_WIDTH = 1024
A_GROUP_DIM = A_WIDTH // A_GROUPS
B_WIDTH = 1024
SHORT_CONV = 3
ATTN_HEADS = 8
ATTN_HEAD_DIM = 128
C_WIDTH = ATTN_HEADS * ATTN_HEAD_DIM
IDX_HEADS = 16
IDX_HEAD_DIM = 64
TOPK_MAX = 256
Q_BLOCK = 128
NUM_BUCKETS = 32
MAX_DISTANCE = 128
NEG_INF = -1e30

SECTION_SIZES = (
    A_WIDTH, A_WIDTH, A_WIDTH,
    B_WIDTH, B_WIDTH, B_WIDTH, B_WIDTH,
    C_WIDTH, C_WIDTH, C_WIDTH, C_WIDTH,
    IDX_HEADS * IDX_HEAD_DIM, IDX_HEAD_DIM, IDX_HEADS,
    D_MODEL, D_MODEL, D_MODEL,
)
IN_COLS = 18512

kernel_name = "hybrid_gated_sgu_shortconv_dsa"


def rmsnorm(x, g):
    xf = x.astype(jnp.float32)
    y = xf * lax.rsqrt(jnp.mean(xf * xf, axis=-1, keepdims=True) + EPS)
    return (y * g.astype(jnp.float32)).astype(x.dtype)


def t5_bucket(dist):
    n = jnp.maximum(dist, 0)
    max_exact = NUM_BUCKETS // 2
    nf = jnp.maximum(n, 1).astype(jnp.float32)
    large = max_exact + (jnp.log(nf / max_exact) / math.log(MAX_DISTANCE / max_exact)
                         * (NUM_BUCKETS - max_exact)).astype(jnp.int32)
    large = jnp.minimum(large, NUM_BUCKETS - 1)
    return jnp.where(n < max_exact, n, large)


def split_columns(p):
    outs = []
    off = 0
    for size in SECTION_SIZES:
        outs.append(p[..., off:off + size])
        off += size
    return outs


def spatial_gating(u, v, ws, bias):
    b_, L, _ = v.shape
    n_chunks = L // CHUNK
    mask = jnp.tril(jnp.ones((CHUNK, CHUNK), dtype=bool))
    ws = jnp.where(mask[None], ws, jnp.zeros_like(ws))
    v5 = v.reshape(b_, n_chunks, CHUNK, A_GROUPS, A_GROUP_DIM)
    mixed = jnp.einsum('gts,bnsgc->bntgc', ws, v5) + bias.T[None, None, :, :, None]
    return u * mixed.reshape(b_, L, A_WIDTH)


def short_gated_conv(gate_b, gate_c, x_in, conv_w):
    cx = gate_c * x_in
    conv = lax.conv_general_dilated(
        cx, conv_w[:, None, :].astype(cx.dtype), window_strides=(1,),
        padding=[(SHORT_CONV - 1, 0)], dimension_numbers=('NWC', 'WIO', 'NWC'),
        feature_group_count=B_WIDTH)
    return gate_b * conv


def dsa_attention(q, k, v, iq, ik, iw, rel_bias, topk):
    b_, L, _ = q.shape
    nb = L // Q_BLOCK
    qb_all = q.reshape(b_, nb, Q_BLOCK, ATTN_HEADS, ATTN_HEAD_DIM).swapaxes(0, 1)
    iq_all = iq.reshape(b_, nb, Q_BLOCK, IDX_HEADS, IDX_HEAD_DIM).swapaxes(0, 1)
    iw_all = (iw * (IDX_HEADS ** -0.5 * IDX_HEAD_DIM ** -0.5)).reshape(
        b_, nb, Q_BLOCK, IDX_HEADS).swapaxes(0, 1)
    k4 = k.reshape(b_, L, ATTN_HEADS, ATTN_HEAD_DIM)
    v4 = v.reshape(b_, L, ATTN_HEADS, ATTN_HEAD_DIM)
    key_pos = jnp.arange(L, dtype=jnp.int32)
    starts = jnp.arange(nb, dtype=jnp.int32) * Q_BLOCK
    scale = ATTN_HEAD_DIM ** -0.5
    gather = jax.vmap(lambda tb, ib: tb[ib])

    def block(args):
        qb, iqb, iwb, start = args
        qpos = start + jnp.arange(Q_BLOCK, dtype=jnp.int32)
        rel = jax.nn.relu(jnp.einsum('bqhd,bsd->bqsh', iqb, ik))
        score = jnp.einsum('bqsh,bqh->bqs', rel, iwb).astype(jnp.float32)
        causal = key_pos[None, None, :] <= qpos[None, :, None]
        score = jnp.where(causal, score, NEG_INF)
        _, idx = lax.top_k(score, topk)
        k_sel = gather(k4, idx)
        v_sel = gather(v4, idx)
        dist = qpos[None, :, None] - idx
        bias = rel_bias[t5_bucket(dist)].astype(jnp.float32)
        logits = jnp.einsum('bqhd,bqkhd->bqhk', qb, k_sel).astype(jnp.float32) * scale
        logits = logits + jnp.transpose(bias, (0, 1, 3, 2))
        logits = jnp.where((dist >= 0)[:, :, None, :], logits, NEG_INF)
        p = jax.nn.softmax(logits, axis=-1)
        return jnp.einsum('bqhk,bqkhd->bqhd', p.astype(v_sel.dtype), v_sel)

    out = lax.map(block, (qb_all, iq_all, iw_all, starts))
    return out.swapaxes(0, 1).reshape(b_, L, C_WIDTH)


def setup_inputs(seed: int = 0) -> dict:
    key = jax.random.key(seed)
    ks = jax.random.split(key, 14)
    f32 = jnp.float32
    nrm = lambda k, shape, s: jax.random.normal(k, shape, f32) * s
    return {
        "x": nrm(ks[0], (BATCH, SEQ, D_MODEL), 1.0),
        "norm_g": 1.0 + nrm(ks[1], (DEPTH, D_MODEL), 0.02),
        "w_in": nrm(ks[2], (DEPTH, D_MODEL, IN_COLS), D_MODEL ** -0.5),
        "a_ws": nrm(ks[3], (DEPTH, A_GROUPS, CHUNK, CHUNK), CHUNK ** -0.5),
        "a_b": 1.0 + nrm(ks[4], (DEPTH, A_GROUPS, CHUNK), 0.1),
        "b_conv": nrm(ks[5], (DEPTH, SHORT_CONV, B_WIDTH), SHORT_CONV ** -0.5),
        "p_a": nrm(ks[6], (DEPTH, A_WIDTH, D_MODEL), A_WIDTH ** -0.5),
        "p_b": nrm(ks[7], (DEPTH, B_WIDTH, D_MODEL), B_WIDTH ** -0.5),
        "p_c": nrm(ks[8], (DEPTH, C_WIDTH, D_MODEL), C_WIDTH ** -0.5),
        "w_o": nrm(ks[9], (DEPTH, D_MODEL, D_MODEL), D_MODEL ** -0.5),
        "rel_bias": nrm(ks[10], (NUM_BUCKETS, ATTN_HEADS), 0.5),
        "final_g": 1.0 + nrm(ks[11], (D_MODEL,), 0.02),
    }


def reference(x, norm_g, w_in, a_ws, a_b, b_conv, p_a, p_b, p_c, w_o, rel_bias, final_g):
    L = x.shape[1]
    topk = min(TOPK_MAX, L // 4)
    for l in range(DEPTH):
        h = rmsnorm(x, norm_g[l])
        proj = jnp.einsum('bsd,de->bse', h, w_in[l])
        (a_u, a_v, a_z, b_bg, b_cg, b_x, b_z, c_q, c_k, c_v, c_z,
         i_q, i_k, i_w, g_a, g_b, g_c) = split_columns(proj)
        y_a = spatial_gating(a_u, a_v, a_ws[l], a_b[l]) * jax.nn.silu(a_z)
        y_b = short_gated_conv(b_bg, b_cg, b_x, b_conv[l]) * jax.nn.silu(b_z)
        y_c = dsa_attention(c_q, c_k, c_v, i_q, i_k, i_w, rel_bias, topk) * jax.nn.silu(c_z)
        m = (jax.nn.sigmoid(g_a) * (y_a @ p_a[l])
             + jax.nn.sigmoid(g_b) * (y_b @ p_b[l])
             + jax.nn.sigmoid(g_c) * (y_c @ p_c[l]))
        x = x + m @ w_o[l]
    return rmsnorm(x, final_g)
```

```python
import functools
import math

import jax
import jax.numpy as jnp
from jax import lax
from jax.experimental import pallas as pl
from jax.experimental.pallas import tpu as pltpu

EPS = 1e-6
CHUNK = 128
A_GROUPS = 8
A_WIDTH = 1024
B_WIDTH = 1024
SHORT_CONV = 3
ATTN_HEADS = 8
ATTN_HEAD_DIM = 128
C_WIDTH = ATTN_HEADS * ATTN_HEAD_DIM
IDX_HEADS = 16
IDX_HEAD_DIM = 64
TOPK_MAX = 256
NUM_BUCKETS = 32
MAX_DISTANCE = 128
NEG_INF = -1e30
SEC = 1024

V7X_LANES = 128
V7X_VMEM_BYTES = 64 * 1024 * 1024
VMEM_LIMIT = V7X_VMEM_BYTES - 8 * 1024 * 1024

F32 = jnp.float32
BF16 = jnp.bfloat16
I32 = jnp.int32
INT_MIN = -(2 ** 31)


def _params(n_axes):
    return pltpu.CompilerParams(
        dimension_semantics=("arbitrary",) * n_axes, vmem_limit_bytes=VMEM_LIMIT)


def _sigmoid(x):
    return 1.0 / (1.0 + jnp.exp(-x))


def _silu(x):
    return x * _sigmoid(x)


def _resident(block_shape, index_map):
    return pl.BlockSpec(block_shape, index_map, pipeline_mode=pl.Buffered(1))


def _rmsnorm_kernel(x_ref, g_ref, o_ref):
    x = x_ref[...]
    y = x * lax.rsqrt(jnp.mean(x * x, axis=-1, keepdims=True) + EPS)
    o_ref[...] = (y * g_ref[...]).astype(o_ref.dtype)


def _rmsnorm(x, g, out_dtype, tm=512):
    m, d = x.shape
    return pl.pallas_call(
        _rmsnorm_kernel,
        grid=(m // tm,),
        in_specs=[pl.BlockSpec((tm, d), lambda i: (i, 0)),
                  pl.BlockSpec((1, d), lambda i: (0, 0))],
        out_specs=pl.BlockSpec((tm, d), lambda i: (i, 0)),
        out_shape=jax.ShapeDtypeStruct((m, d), out_dtype),
        compiler_params=_params(1),
        name="rmsnorm",
    )(x, g.reshape(1, d))


def _matmul_kernel(a_ref, w_ref, o_ref):
    o_ref[...] = jnp.dot(a_ref[...], w_ref[...], preferred_element_type=F32).astype(o_ref.dtype)


def _matmul(a, w, out_dtype, tm, tn, name):
    m, k = a.shape
    n = w.shape[1]
    tm = min(tm, m)
    assert m % tm == 0 and n % tn == 0
    return pl.pallas_call(
        _matmul_kernel,
        grid=(m // tm, n // tn),
        in_specs=[pl.BlockSpec((tm, k), lambda i, j: (i, 0)),
                  pl.BlockSpec((k, tn), lambda i, j: (0, j))],
        out_specs=pl.BlockSpec((tm, tn), lambda i, j: (i, j)),
        out_shape=jax.ShapeDtypeStruct((m, n), out_dtype),
        compiler_params=_params(2),
        name=name,
    )(a, w)


def _mix_a_kernel(u_ref, v_ref, z_ref, ws_ref, bt_ref, o_ref, *, n_chunks):
    ti = lax.broadcasted_iota(I32, (CHUNK, CHUNK), 0)
    si = lax.broadcasted_iota(I32, (CHUNK, CHUNK), 1)
    tril = si <= ti
    gd = A_WIDTH // A_GROUPS
    for g in range(A_GROUPS):
        w = jnp.where(tril, ws_ref[g], 0.0).astype(BF16)
        b = bt_ref[:, g:g + 1]
        for n in range(n_chunks):
            rows = slice(n * CHUNK, (n + 1) * CHUNK)
            cols = slice(g * gd, (g + 1) * gd)
            mixed = jnp.dot(w, v_ref[rows, cols], preferred_element_type=F32) + b
            u = u_ref[rows, cols].astype(F32)
            z = z_ref[rows, cols].astype(F32)
            o_ref[rows, cols] = (u * mixed * _silu(z)).astype(o_ref.dtype)


def _mix_a(p_main, ws, bias_t, sec0, tm=512):
    m = p_main.shape[0]
    col = lambda c: pl.BlockSpec((tm, SEC), lambda i, c=c: (i, sec0 + c))
    return pl.pallas_call(
        functools.partial(_mix_a_kernel, n_chunks=tm // CHUNK),
        grid=(m // tm,),
        in_specs=[col(0), col(1), col(2),
                  pl.BlockSpec(ws.shape, lambda i: (0, 0, 0)),
                  pl.BlockSpec(bias_t.shape, lambda i: (0, 0))],
        out_specs=pl.BlockSpec((tm, SEC), lambda i: (i, 0)),
        out_shape=jax.ShapeDtypeStruct((m, A_WIDTH), BF16),
        compiler_params=_params(1),
        name="mix_a",
    )(p_main, p_main, p_main, ws, bias_t)


HALO = 16


def _mix_b_kernel(bg_ref, cg_ref, x_ref, z_ref, cgh_ref, xh_ref, w_ref, o_ref, *, tiles_per_seq):
    i = pl.program_id(0)
    tm = bg_ref.shape[0]
    cx = cg_ref[...].astype(F32) * x_ref[...].astype(F32)
    halo = cgh_ref[...].astype(F32) * xh_ref[...].astype(F32)
    halo = jnp.where(i % tiles_per_seq == 0, 0.0, halo)
    row = lax.broadcasted_iota(I32, cx.shape, 0)
    cx1 = pltpu.roll(cx, 1, axis=0)
    cx1 = jnp.where(row == 0, halo[HALO - 1:HALO, :], cx1)
    cx2 = pltpu.roll(cx, 2, axis=0)
    cx2 = jnp.where(row == 0, halo[HALO - 2:HALO - 1, :], cx2)
    cx2 = jnp.where(row == 1, halo[HALO - 1:HALO, :], cx2)
    conv = w_ref[0:1, :] * cx2 + w_ref[1:2, :] * cx1 + w_ref[2:3, :] * cx
    y = bg_ref[...].astype(F32) * conv * _silu(z_ref[...].astype(F32))
    o_ref[...] = y.astype(o_ref.dtype)


def _mix_b(p_main, conv_w, sec0, seq, tm=512):
    m = p_main.shape[0]
    col = lambda c: pl.BlockSpec((tm, SEC), lambda i, c=c: (i, sec0 + c))
    halo = lambda c: pl.BlockSpec(
        (HALO, SEC), lambda i, c=c: (jnp.maximum(i * (tm // HALO) - 1, 0), sec0 + c))
    return pl.pallas_call(
        functools.partial(_mix_b_kernel, tiles_per_seq=seq // tm),
        grid=(m // tm,),
        in_specs=[col(3), col(4), col(5), col(6), halo(4), halo(5),
                  pl.BlockSpec(conv_w.shape, lambda i: (0, 0))],
        out_specs=pl.BlockSpec((tm, SEC), lambda i: (i, 0)),
        out_shape=jax.ShapeDtypeStruct((m, B_WIDTH), BF16),
        compiler_params=_params(1),
        name="mix_b",
    )(p_main, p_main, p_main, p_main, p_main, p_main, conv_w)


def _t5_thresholds():
    max_exact = NUM_BUCKETS // 2
    def bucket(n):
        v = math.log(n / max_exact) / math.log(MAX_DISTANCE / max_exact) * (NUM_BUCKETS - max_exact)
        return max_exact + int(v)
    out = []
    for k in range(max_exact + 1, NUM_BUCKETS):
        out.append(next(n for n in range(max_exact, 4 * MAX_DISTANCE) if bucket(n) >= k))
    return tuple(out)


def _sortable(x):
    bits = pltpu.bitcast(x, I32)
    return bits ^ ((bits >> 31) & jnp.int32(0x7FFFFFFF))


def _dsa_kernel(relb_ref, iq_ref, iwq_ref, ik_ref, q_ref, k_ref, v_ref, cz_ref, o_ref,
                key_s, mb_s, wb_s, bias_s, m_s, l_s, acc_s, *, tq, topk):
    qi = pl.program_id(1)
    n_kc = qi + 1
    q0 = qi * tq
    ri = lax.broadcasted_iota(I32, (tq, tq), 0)
    ci = lax.broadcasted_iota(I32, (tq, tq), 1)

    @pl.when((pl.program_id(0) == 0) & (qi == 0))
    def _():
        thresholds = _t5_thresholds()
        for which in range(3):
            n = jnp.maximum(ri - ci + which * tq, 0) if which < 2 else jnp.full((tq, tq), 2 * MAX_DISTANCE, I32)
            bucket = jnp.full((tq, tq), NUM_BUCKETS // 2, I32)
            for t in thresholds:
                bucket = bucket + (n >= t).astype(I32)
            bucket = jnp.where(n < NUM_BUCKETS // 2, n, bucket)
            for h in range(ATTN_HEADS):
                def fill(b, acc):
                    return jnp.where(bucket == b, relb_ref[b, h], acc)
                bias_s[which, h] = lax.fori_loop(0, NUM_BUCKETS, fill, jnp.zeros((tq, tq), F32))

    w_scale = IDX_HEADS ** -0.5 * IDX_HEAD_DIM ** -0.5
    iw = iwq_ref[:, IDX_HEAD_DIM:IDX_HEAD_DIM + IDX_HEADS] * w_scale
    for h in range(IDX_HEADS):
        wb_s[h] = jnp.broadcast_to(iw[:, h:h + 1], (tq, tq))
    iq = iq_ref[...].astype(BF16)

    def score_tile(c, carry):
        ks = pl.multiple_of(c * tq, tq)
        kc = ik_ref[pl.ds(ks, tq), :][:, :IDX_HEAD_DIM].astype(BF16)
        acc = jnp.zeros((tq, tq), F32)
        for h in range(IDX_HEADS):
            z = lax.dot_general(iq[:, h * IDX_HEAD_DIM:(h + 1) * IDX_HEAD_DIM], kc,
                                (((1,), (1,)), ((), ())), preferred_element_type=F32)
            acc = acc + jnp.maximum(z, 0.0) * wb_s[h]
        causal = (ci + (c - qi) * tq) <= ri
        key_s[c] = _sortable(jnp.where(causal, acc, NEG_INF))
        return carry

    lax.fori_loop(0, n_kc, score_tile, 0)

    def count_ge(cand):
        def body(c, cnt):
            ge = (key_s[c] >= cand).astype(I32)
            for j in range(tq // V7X_LANES):
                cnt = cnt + ge[:, j * V7X_LANES:(j + 1) * V7X_LANES]
            return cnt
        cnt = lax.fori_loop(0, n_kc, body, jnp.zeros((tq, V7X_LANES), I32))
        return jnp.sum(cnt, axis=1, keepdims=True)

    def select_bit(i, thr):
        cand = thr + jnp.left_shift(jnp.int32(1), 31 - i)
        return jnp.where(count_ge(cand) >= topk, cand, thr)

    thr = lax.fori_loop(0, 32, select_bit, jnp.full((tq, 1), INT_MIN, I32))

    def mask_tile(c, carry):
        causal = (ci + (c - qi) * tq) <= ri
        mb_s[c] = jnp.where((key_s[c] >= thr) & causal, 0.0, NEG_INF)
        return carry

    lax.fori_loop(0, n_kc, mask_tile, 0)

    scale = ATTN_HEAD_DIM ** -0.5
    for h in range(ATTN_HEADS):
        hs = slice(h * ATTN_HEAD_DIM, (h + 1) * ATTN_HEAD_DIM)
        qh = q_ref[:, hs]
        m_s[...] = jnp.full(m_s.shape, NEG_INF, F32)
        l_s[...] = jnp.zeros(l_s.shape, F32)
        acc_s[...] = jnp.zeros(acc_s.shape, F32)

        def attend(c, carry):
            ks = pl.multiple_of(c * tq, tq)
            kh = k_ref[pl.ds(ks, tq), hs]
            vh = v_ref[pl.ds(ks, tq), hs]
            s = lax.dot_general(qh, kh, (((1,), (1,)), ((), ())), preferred_element_type=F32) * scale
            s = s + bias_s[jnp.minimum(qi - c, 2), h] + mb_s[c]
            m_prev = m_s[...]
            m_new = jnp.maximum(m_prev, jnp.max(s, axis=1, keepdims=True))
            alpha = jnp.exp(m_prev - m_new)
            p = jnp.exp(s - m_new)
            l_s[...] = alpha * l_s[...] + jnp.sum(p, axis=1, keepdims=True)
            acc_s[...] = alpha * acc_s[...] + jnp.dot(p.astype(BF16), vh, preferred_element_type=F32)
            m_s[...] = m_new
            return carry

        lax.fori_loop(0, n_kc, attend, 0)
        out = acc_s[...] / l_s[...]
        o_ref[:, hs] = (out * _silu(cz_ref[:, hs].astype(F32))).astype(o_ref.dtype)


def _dsa(p_main, p_idx, rel_bias, sec0, batch, seq, topk, tq=256):
    m = p_main.shape[0]
    nq = seq // tq
    idx_cols = IDX_HEADS * IDX_HEAD_DIM // V7X_LANES
    row = lambda b, q: b * nq + q
    kernel = functools.partial(_dsa_kernel, tq=tq, topk=topk)
    return pl.pallas_call(
        kernel,
        grid=(batch, nq),
        in_specs=[
            pl.BlockSpec(memory_space=pltpu.SMEM),
            pl.BlockSpec((tq, IDX_HEADS * IDX_HEAD_DIM), lambda b, q: (row(b, q), 0)),
            pl.BlockSpec((tq, V7X_LANES), lambda b, q: (row(b, q), idx_cols)),
            _resident((seq, V7X_LANES), lambda b, q: (b, idx_cols)),
            pl.BlockSpec((tq, SEC), lambda b, q: (row(b, q), sec0 + 7)),
            _resident((seq, SEC), lambda b, q: (b, sec0 + 8)),
            _resident((seq, SEC), lambda b, q: (b, sec0 + 9)),
            pl.BlockSpec((tq, SEC), lambda b, q: (row(b, q), sec0 + 10)),
        ],
        out_specs=pl.BlockSpec((tq, SEC), lambda b, q: (row(b, q), 0)),
        out_shape=jax.ShapeDtypeStruct((m, C_WIDTH), BF16),
        scratch_shapes=[
            pltpu.VMEM((nq, tq, tq), I32),
            pltpu.VMEM((nq, tq, tq), F32),
            pltpu.VMEM((IDX_HEADS, tq, tq), F32),
            pltpu.VMEM((3, ATTN_HEADS, tq, tq), F32),
            pltpu.VMEM((tq, 1), F32),
            pltpu.VMEM((tq, 1), F32),
            pltpu.VMEM((tq, ATTN_HEAD_DIM), F32),
        ],
        compiler_params=_params(2),
        name="dsa",
    )(rel_bias, p_idx, p_idx, p_idx, p_main, p_main, p_main, p_main)


def _merge_kernel(ya_ref, yb_ref, yc_ref, ga_ref, gb_ref, gc_ref, x_ref,
                  pa_ref, pb_ref, pc_ref, wo_ref, o_ref):
    def branch(y_ref, g_ref, p_ref):
        return _sigmoid(g_ref[...].astype(F32)) * jnp.dot(y_ref[...], p_ref[...], preferred_element_type=F32)
    mix = branch(ya_ref, ga_ref, pa_ref) + branch(yb_ref, gb_ref, pb_ref) + branch(yc_ref, gc_ref, pc_ref)
    o_ref[...] = x_ref[...] + jnp.dot(mix.astype(BF16), wo_ref[...], preferred_element_type=F32)


def _merge(y_a, y_b, y_c, p_main, x, p_a, p_b, p_c, w_o, tm=256):
    m, d = x.shape
    ytile = pl.BlockSpec((tm, SEC), lambda i: (i, 0))
    gate = lambda c: pl.BlockSpec((tm, d), lambda i, c=c: (i, c))
    return pl.pallas_call(
        _merge_kernel,
        grid=(m // tm,),
        in_specs=[ytile, ytile, ytile, gate(0), gate(1), gate(2),
                  pl.BlockSpec((tm, d), lambda i: (i, 0)),
                  _resident((SEC, d), lambda i: (0, 0)),
                  _resident((SEC, d), lambda i: (0, 0)),
                  _resident((SEC, d), lambda i: (0, 0)),
                  _resident((d, d), lambda i: (0, 0))],
        out_specs=pl.BlockSpec((tm, d), lambda i: (i, 0)),
        out_shape=jax.ShapeDtypeStruct((m, d), F32),
        compiler_params=_params(1),
        name="merge",
    )(y_a, y_b, y_c, p_main, p_main, p_main, x, p_a, p_b, p_c, w_o)


def kernel(x, norm_g, w_in, a_ws, a_b, b_conv, p_a, p_b, p_c, w_o, rel_bias, final_g):
    batch, seq, d = x.shape
    depth = w_in.shape[0]
    m = batch * seq
    topk = min(TOPK_MAX, seq // 4)
    n_mix = 11 * SEC
    n_idx = IDX_HEADS * IDX_HEAD_DIM + IDX_HEAD_DIM + IDX_HEADS
    assert w_in.shape[2] == n_mix + n_idx + 3 * d and (3 * d) % SEC == 0
    sec0 = 3 * d // SEC
    idx_pad = -n_idx % V7X_LANES

    w_main = jnp.concatenate([w_in[:, :, n_mix + n_idx:], w_in[:, :, :n_mix]], axis=-1).astype(BF16)
    w_idx = jnp.pad(w_in[:, :, n_mix:n_mix + n_idx], ((0, 0), (0, 0), (0, idx_pad))).astype(BF16)
    a_bt = jnp.swapaxes(a_b, 1, 2)
    p_a16, p_b16, p_c16, w_o16 = (w.astype(BF16) for w in (p_a, p_b, p_c, w_o))

    xf = x.reshape(m, d)
    for l in range(depth):
        h = _rmsnorm(xf, norm_g[l], BF16)
        p_main = _matmul(h, w_main[l], BF16, tm=1024, tn=512, name="proj_main")
        p_idx = _matmul(h, w_idx[l], F32, tm=1024, tn=w_idx.shape[2], name="proj_idx")
        y_a = _mix_a(p_main, a_ws[l], a_bt[l], sec0)
        y_b = _mix_b(p_main, b_conv[l], sec0, seq)
        y_c = _dsa(p_main, p_idx, rel_bias, sec0, batch, seq, topk)
        xf = _merge(y_a, y_b, y_c, p_main, xf, p_a16[l], p_b16[l], p_c16[l], w_o16[l])
    return _rmsnorm(xf, final_g, x.dtype).reshape(batch, seq, d)
```

```python
import functools
import math

import jax
import jax.numpy as jnp
from jax import lax
from jax.experimental import pallas as pl
from jax.experimental.pallas import tpu as pltpu

EPS = 1e-6
CHUNK = 128
A_GROUPS = 8
A_WIDTH = 1024
B_WIDTH = 1024
SHORT_CONV = 3
ATTN_HEADS = 8
ATTN_HEAD_DIM = 128
C_WIDTH = ATTN_HEADS * ATTN_HEAD_DIM
IDX_HEADS = 16
IDX_HEAD_DIM = 64
TOPK_MAX = 256
NUM_BUCKETS = 32
MAX_DISTANCE = 128
NEG_INF = -1e30
SEC = 1024
HEADS_AHEAD = 3

V7X_LANES = 128
V7X_VMEM_BYTES = 64 * 1024 * 1024
VMEM_LIMIT = V7X_VMEM_BYTES - 8 * 1024 * 1024

F32 = jnp.float32
BF16 = jnp.bfloat16
I32 = jnp.int32
INT_MIN = -(2 ** 31)


def _params(n_axes):
    return pltpu.CompilerParams(
        dimension_semantics=("arbitrary",) * n_axes, vmem_limit_bytes=VMEM_LIMIT)


def _sigmoid(x):
    return 1.0 / (1.0 + jnp.exp(-x))


def _silu(x):
    return x * _sigmoid(x)


def _resident(block_shape, index_map):
    return pl.BlockSpec(block_shape, index_map, pipeline_mode=pl.Buffered(1))


def _rmsnorm_kernel(x_ref, g_ref, o_ref):
    x = x_ref[...]
    y = x * lax.rsqrt(jnp.mean(x * x, axis=-1, keepdims=True) + EPS)
    o_ref[...] = (y * g_ref[...]).astype(o_ref.dtype)


def _rmsnorm(x, g, out_dtype, tm=512):
    m, d = x.shape
    return pl.pallas_call(
        _rmsnorm_kernel,
        grid=(m // tm,),
        in_specs=[pl.BlockSpec((tm, d), lambda i: (i, 0)),
                  pl.BlockSpec((1, d), lambda i: (0, 0))],
        out_specs=pl.BlockSpec((tm, d), lambda i: (i, 0)),
        out_shape=jax.ShapeDtypeStruct((m, d), out_dtype),
        compiler_params=_params(1),
        name="rmsnorm",
    )(x, g.reshape(1, d))


def _matmul_kernel(a_ref, w_ref, o_ref):
    o_ref[...] = jnp.dot(a_ref[...], w_ref[...], preferred_element_type=F32).astype(o_ref.dtype)


def _matmul(a, w, out_dtype, tm, tn, name):
    m, k = a.shape
    n = w.shape[1]
    tm = min(tm, m)
    assert m % tm == 0 and n % tn == 0
    return pl.pallas_call(
        _matmul_kernel,
        grid=(m // tm, n // tn),
        in_specs=[pl.BlockSpec((tm, k), lambda i, j: (i, 0)),
                  pl.BlockSpec((k, tn), lambda i, j: (0, j))],
        out_specs=pl.BlockSpec((tm, tn), lambda i, j: (i, j)),
        out_shape=jax.ShapeDtypeStruct((m, n), out_dtype),
        compiler_params=_params(2),
        name=name,
    )(a, w)


def _mix_a_kernel(u_ref, v_ref, z_ref, ws_ref, bt_ref, o_ref, *, n_chunks):
    ti = lax.broadcasted_iota(I32, (CHUNK, CHUNK), 0)
    si = lax.broadcasted_iota(I32, (CHUNK, CHUNK), 1)
    tril = si <= ti
    gd = A_WIDTH // A_GROUPS
    for g in range(A_GROUPS):
        w = jnp.where(tril, ws_ref[g], 0.0).astype(BF16)
        b = bt_ref[:, g:g + 1]
        for n in range(n_chunks):
            rows = slice(n * CHUNK, (n + 1) * CHUNK)
            cols = slice(g * gd, (g + 1) * gd)
            mixed = jnp.dot(w, v_ref[rows, cols], preferred_element_type=F32) + b
            u = u_ref[rows, cols].astype(F32)
            z = z_ref[rows, cols].astype(F32)
            o_ref[rows, cols] = (u * mixed * _silu(z)).astype(o_ref.dtype)


def _mix_a(p_main, ws, bias_t, sec0, tm=512):
    m = p_main.shape[0]
    col = lambda c: pl.BlockSpec((tm, SEC), lambda i, c=c: (i, sec0 + c))
    return pl.pallas_call(
        functools.partial(_mix_a_kernel, n_chunks=tm // CHUNK),
        grid=(m // tm,),
        in_specs=[col(0), col(1), col(2),
                  pl.BlockSpec(ws.shape, lambda i: (0, 0, 0)),
                  pl.BlockSpec(bias_t.shape, lambda i: (0, 0))],
        out_specs=pl.BlockSpec((tm, SEC), lambda i: (i, 0)),
        out_shape=jax.ShapeDtypeStruct((m, A_WIDTH), BF16),
        compiler_params=_params(1),
        name="mix_a",
    )(p_main, p_main, p_main, ws, bias_t)


HALO = 16


def _mix_b_kernel(bg_ref, cg_ref, x_ref, z_ref, cgh_ref, xh_ref, w_ref, o_ref, *, tiles_per_seq):
    i = pl.program_id(0)
    tm = bg_ref.shape[0]
    cx = cg_ref[...].astype(F32) * x_ref[...].astype(F32)
    halo = cgh_ref[...].astype(F32) * xh_ref[...].astype(F32)
    halo = jnp.where(i % tiles_per_seq == 0, 0.0, halo)
    row = lax.broadcasted_iota(I32, cx.shape, 0)
    cx1 = pltpu.roll(cx, 1, axis=0)
    cx1 = jnp.where(row == 0, halo[HALO - 1:HALO, :], cx1)
    cx2 = pltpu.roll(cx, 2, axis=0)
    cx2 = jnp.where(row == 0, halo[HALO - 2:HALO - 1, :], cx2)
    cx2 = jnp.where(row == 1, halo[HALO - 1:HALO, :], cx2)
    conv = w_ref[0:1, :] * cx2 + w_ref[1:2, :] * cx1 + w_ref[2:3, :] * cx
    y = bg_ref[...].astype(F32) * conv * _silu(z_ref[...].astype(F32))
    o_ref[...] = y.astype(o_ref.dtype)


def _mix_b(p_main, conv_w, sec0, seq, tm=512):
    m = p_main.shape[0]
    col = lambda c: pl.BlockSpec((tm, SEC), lambda i, c=c: (i, sec0 + c))
    halo = lambda c: pl.BlockSpec(
        (HALO, SEC), lambda i, c=c: (jnp.maximum(i * (tm // HALO) - 1, 0), sec0 + c))
    return pl.pallas_call(
        functools.partial(_mix_b_kernel, tiles_per_seq=seq // tm),
        grid=(m // tm,),
        in_specs=[col(3), col(4), col(5), col(6), halo(4), halo(5),
                  pl.BlockSpec(conv_w.shape, lambda i: (0, 0))],
        out_specs=pl.BlockSpec((tm, SEC), lambda i: (i, 0)),
        out_shape=jax.ShapeDtypeStruct((m, B_WIDTH), BF16),
        compiler_params=_params(1),
        name="mix_b",
    )(p_main, p_main, p_main, p_main, p_main, p_main, conv_w)


def _t5_thresholds():
    max_exact = NUM_BUCKETS // 2
    def bucket(n):
        v = math.log(n / max_exact) / math.log(MAX_DISTANCE / max_exact) * (NUM_BUCKETS - max_exact)
        return max_exact + int(v)
    out = []
    for k in range(max_exact + 1, NUM_BUCKETS):
        out.append(next(n for n in range(max_exact, 4 * MAX_DISTANCE) if bucket(n) >= k))
    return tuple(out)


def _sortable(x):
    bits = pltpu.bitcast(x, I32)
    return bits ^ ((bits >> 31) & jnp.int32(0x7FFFFFFF))


def _dsa_kernel(relb_ref, iq_ref, iwq_ref, ik_ref, q_ref, k_ref, v_ref, cz_ref, o_ref,
                key_s, mb_s, wt_s, bias_s, m_s, l_s, acc_s, s_s, *, tq, topk):
    qi = pl.program_id(1)
    n_kc = qi + 1
    kj = lax.broadcasted_iota(I32, (tq, tq), 0)
    qj = lax.broadcasted_iota(I32, (tq, tq), 1)
    log2e = math.log2(math.e)

    def causal(c):
        return (kj + (c - qi) * tq) <= qj

    @pl.when((pl.program_id(0) == 0) & (qi == 0))
    def _():
        thresholds = _t5_thresholds()
        for which in range(2):
            n = jnp.maximum(qj - kj + which * tq, 0)
            bucket = jnp.full((tq, tq), NUM_BUCKETS // 2, I32)
            for t in thresholds:
                bucket = bucket + (n >= t).astype(I32)
            bucket = jnp.where(n < NUM_BUCKETS // 2, n, bucket)
            for h in range(ATTN_HEADS):
                def fill(b, acc):
                    return jnp.where(bucket == b, relb_ref[b, h], acc)
                tile = lax.fori_loop(0, NUM_BUCKETS, fill, jnp.zeros((tq, tq), F32))
                bias_s[which, h] = (tile - relb_ref[NUM_BUCKETS - 1, h]) * log2e

    w_scale = IDX_HEADS ** -0.5 * IDX_HEAD_DIM ** -0.5
    wt_s[...] = (iwq_ref[...] * w_scale).T
    iq = iq_ref[...].astype(BF16)

    def score_tile(c, carry):
        ks = pl.multiple_of(c * tq, tq)
        kc = ik_ref[pl.ds(ks, tq), :][:, :IDX_HEAD_DIM].astype(BF16)
        acc = jnp.zeros((tq, tq), F32)
        for h in range(IDX_HEADS):
            z = lax.dot_general(kc, iq[:, h * IDX_HEAD_DIM:(h + 1) * IDX_HEAD_DIM],
                                (((1,), (1,)), ((), ())), preferred_element_type=F32)
            acc = acc + jnp.maximum(z, 0.0) * wt_s[IDX_HEAD_DIM + h:IDX_HEAD_DIM + h + 1, :]
        key_s[c] = _sortable(jnp.where(causal(c), acc, NEG_INF))
        return carry

    lax.fori_loop(0, n_kc, score_tile, 0)

    def count_ge(cand):
        def body(c, cnt):
            ge = (key_s[c] >= cand).astype(I32)
            return cnt + jnp.sum(ge.reshape(tq // 8, 8, tq), axis=0)
        cnt = lax.fori_loop(0, n_kc, body, jnp.zeros((8, tq), I32))
        return jnp.sum(cnt, axis=0, keepdims=True)

    def select_bit(i, thr):
        cand = thr + jnp.left_shift(jnp.int32(1), 31 - i)
        return jnp.where(count_ge(cand) >= topk, cand, thr)

    thr = lax.fori_loop(0, 32, select_bit, jnp.full((1, tq), INT_MIN, I32))

    def mask_tile(c, carry):
        mb_s[c] = jnp.where((key_s[c] >= thr) & causal(c), 0.0, NEG_INF)
        return carry

    lax.fori_loop(0, n_kc, mask_tile, 0)

    scale2 = ATTN_HEAD_DIM ** -0.5 * log2e
    for h in range(ATTN_HEADS):
        m_s[h] = jnp.full((1, tq), NEG_INF, F32)
        l_s[h] = jnp.zeros((1, tq), F32)
        acc_s[h] = jnp.zeros((ATTN_HEAD_DIM, tq), F32)

    def attend(c, which):
        ks = pl.multiple_of(c * tq, tq)

        def logits(h):
            hs = slice(h * ATTN_HEAD_DIM, (h + 1) * ATTN_HEAD_DIM)
            kh = k_ref[pl.ds(ks, tq), hs]
            s = lax.dot_general(kh, q_ref[:, hs], (((1,), (1,)), ((), ())),
                                preferred_element_type=F32) * scale2 + mb_s[c]
            if which is not None:
                s = s + bias_s[which, h]
            s_s[h % (HEADS_AHEAD + 1)] = s
            return jnp.max(s, axis=0, keepdims=True)

        def accumulate(h, m_cur):
            hs = slice(h * ATTN_HEAD_DIM, (h + 1) * ATTN_HEAD_DIM)
            vh = v_ref[pl.ds(ks, tq), hs]
            m_prev = m_s[h]
            m_new = jnp.maximum(m_prev, m_cur)
            alpha = jnp.exp2(m_prev - m_new)
            p = jnp.exp2(s_s[h % (HEADS_AHEAD + 1)] - m_new)
            l_s[h] = alpha * l_s[h] + jnp.sum(p, axis=0, keepdims=True)
            pv = lax.dot_general(vh, p.astype(BF16), (((0,), (0,)), ((), ())),
                                 preferred_element_type=F32)
            acc_s[h] = alpha * acc_s[h] + pv
            m_s[h] = m_new

        m_cur = [logits(h) for h in range(HEADS_AHEAD)]
        for h in range(ATTN_HEADS):
            if h + HEADS_AHEAD < ATTN_HEADS:
                m_cur.append(logits(h + HEADS_AHEAD))
            accumulate(h, m_cur[h])

    def attend_far(c, carry):
        attend(c, None)
        return carry

    lax.fori_loop(0, jnp.maximum(qi - 1, 0), attend_far, 0)

    @pl.when(qi >= 1)
    def _():
        attend(qi - 1, 1)

    attend(qi, 0)

    for h in range(ATTN_HEADS):
        hs = slice(h * ATTN_HEAD_DIM, (h + 1) * ATTN_HEAD_DIM)
        out = (acc_s[h] / l_s[h]).T
        o_ref[:, hs] = (out * _silu(cz_ref[:, hs].astype(F32))).astype(o_ref.dtype)


def _dsa(p_main, p_idx, rel_bias, sec0, batch, seq, topk, tq=256):
    m = p_main.shape[0]
    nq = seq // tq
    idx_cols = IDX_HEADS * IDX_HEAD_DIM // V7X_LANES
    row = lambda b, q: b * nq + q
    kernel = functools.partial(_dsa_kernel, tq=tq, topk=topk)
    return pl.pallas_call(
        kernel,
        grid=(batch, nq),
        in_specs=[
            pl.BlockSpec(memory_space=pltpu.SMEM),
            pl.BlockSpec((tq, IDX_HEADS * IDX_HEAD_DIM), lambda b, q: (row(b, q), 0)),
            pl.BlockSpec((tq, V7X_LANES), lambda b, q: (row(b, q), idx_cols)),
            _resident((seq, V7X_LANES), lambda b, q: (b, idx_cols)),
            pl.BlockSpec((tq, SEC), lambda b, q: (row(b, q), sec0 + 7)),
            _resident((seq, SEC), lambda b, q: (b, sec0 + 8)),
            _resident((seq, SEC), lambda b, q: (b, sec0 + 9)),
            pl.BlockSpec((tq, SEC), lambda b, q: (row(b, q), sec0 + 10)),
        ],
        out_specs=pl.BlockSpec((tq, SEC), lambda b, q: (row(b, q), 0)),
        out_shape=jax.ShapeDtypeStruct((m, C_WIDTH), BF16),
        scratch_shapes=[
            pltpu.VMEM((nq, tq, tq), I32),
            pltpu.VMEM((nq, tq, tq), F32),
            pltpu.VMEM((V7X_LANES, tq), F32),
            pltpu.VMEM((2, ATTN_HEADS, tq, tq), F32),
            pltpu.VMEM((ATTN_HEADS, 1, tq), F32),
            pltpu.VMEM((ATTN_HEADS, 1, tq), F32),
            pltpu.VMEM((ATTN_HEADS, ATTN_HEAD_DIM, tq), F32),
            pltpu.VMEM((HEADS_AHEAD + 1, tq, tq), F32),
        ],
        compiler_params=_params(2),
        name="dsa",
    )(rel_bias, p_idx, p_idx, p_idx, p_main, p_main, p_main, p_main)


def _merge_kernel(ya_ref, yb_ref, yc_ref, ga_ref, gb_ref, gc_ref, x_ref,
                  pa_ref, pb_ref, pc_ref, wo_ref, o_ref):
    def branch(y_ref, g_ref, p_ref):
        return _sigmoid(g_ref[...].astype(F32)) * jnp.dot(y_ref[...], p_ref[...], preferred_element_type=F32)
    mix = branch(ya_ref, ga_ref, pa_ref) + branch(yb_ref, gb_ref, pb_ref) + branch(yc_ref, gc_ref, pc_ref)
    o_ref[...] = x_ref[...] + jnp.dot(mix.astype(BF16), wo_ref[...], preferred_element_type=F32)


def _merge(y_a, y_b, y_c, p_main, x, p_a, p_b, p_c, w_o, tm=256):
    m, d = x.shape
    ytile = pl.BlockSpec((tm, SEC), lambda i: (i, 0))
    gate = lambda c: pl.BlockSpec((tm, d), lambda i, c=c: (i, c))
    return pl.pallas_call(
        _merge_kernel,
        grid=(m // tm,),
        in_specs=[ytile, ytile, ytile, gate(0), gate(1), gate(2),
                  pl.BlockSpec((tm, d), lambda i: (i, 0)),
                  _resident((SEC, d), lambda i: (0, 0)),
                  _resident((SEC, d), lambda i: (0, 0)),
                  _resident((SEC, d), lambda i: (0, 0)),
                  _resident((d, d), lambda i: (0, 0))],
        out_specs=pl.BlockSpec((tm, d), lambda i: (i, 0)),
        out_shape=jax.ShapeDtypeStruct((m, d), F32),
        compiler_params=_params(1),
        name="merge",
    )(y_a, y_b, y_c, p_main, p_main, p_main, x, p_a, p_b, p_c, w_o)


def kernel(x, norm_g, w_in, a_ws, a_b, b_conv, p_a, p_b, p_c, w_o, rel_bias, final_g):
    batch, seq, d = x.shape
    depth = w_in.shape[0]
    m = batch * seq
    topk = min(TOPK_MAX, seq // 4)
    n_mix = 11 * SEC
    n_idx = IDX_HEADS * IDX_HEAD_DIM + IDX_HEAD_DIM + IDX_HEADS
    assert w_in.shape[2] == n_mix + n_idx + 3 * d and (3 * d) % SEC == 0
    sec0 = 3 * d // SEC
    idx_pad = -n_idx % V7X_LANES

    w_main = jnp.concatenate([w_in[:, :, n_mix + n_idx:], w_in[:, :, :n_mix]], axis=-1).astype(BF16)
    w_idx = jnp.pad(w_in[:, :, n_mix:n_mix + n_idx], ((0, 0), (0, 0), (0, idx_pad))).astype(BF16)
    a_bt = jnp.swapaxes(a_b, 1, 2)
    p_a16, p_b16, p_c16, w_o16 = (w.astype(BF16) for w in (p_a, p_b, p_c, w_o))

    xf = x.reshape(m, d)
    for l in range(depth):
        h = _rmsnorm(xf, norm_g[l], BF16)
        p_main = _matmul(h, w_main[l], BF16, tm=1024, tn=512, name="proj_main")
        p_idx = _matmul(h, w_idx[l], F32, tm=1024, tn=w_idx.shape[2], name="proj_idx")
        y_a = _mix_a(p_main, a_ws[l], a_bt[l], sec0)
        y_b = _mix_b(p_main, b_conv[l], sec0, seq)
        y_c = _dsa(p_main, p_idx, rel_bias, sec0, batch, seq, topk)
        xf = _merge(y_a, y_b, y_c, p_main, xf, p_a16[l], p_b16[l], p_c16[l], w_o16[l])
    return _rmsnorm(xf, final_g, x.dtype).reshape(batch, seq, d)
```

```python
import functools
import math

import jax
import jax.numpy as jnp
from jax import lax
from jax.experimental import pallas as pl
from jax.experimental.pallas import tpu as pltpu

EPS = 1e-6
CHUNK = 128
A_GROUPS = 8
A_WIDTH = 1024
B_WIDTH = 1024
SHORT_CONV = 3
ATTN_HEADS = 8
ATTN_HEAD_DIM = 128
C_WIDTH = ATTN_HEADS * ATTN_HEAD_DIM
IDX_HEADS = 16
IDX_HEAD_DIM = 64
TOPK_MAX = 256
NUM_BUCKETS = 32
MAX_DISTANCE = 128
NEG_INF = -1e30
SEC = 1024
RUN_DEPTH = 64
HEADS_AHEAD = 3

V7X_LANES = 128
SUBLANES = 8
V7X_VMEM_BYTES = 64 * 1024 * 1024
VMEM_LIMIT = V7X_VMEM_BYTES - 8 * 1024 * 1024

F32 = jnp.float32
BF16 = jnp.bfloat16
I32 = jnp.int32
INT_MIN = -(2 ** 31)


def _params(n_axes):
    return pltpu.CompilerParams(
        dimension_semantics=("arbitrary",) * n_axes, vmem_limit_bytes=VMEM_LIMIT)


def _sigmoid(x):
    return 1.0 / (1.0 + jnp.exp(-x))


def _silu(x):
    return x * _sigmoid(x)


def _resident(block_shape, index_map):
    return pl.BlockSpec(block_shape, index_map, pipeline_mode=pl.Buffered(1))


def _rmsnorm_kernel(x_ref, g_ref, o_ref):
    x = x_ref[...]
    y = x * lax.rsqrt(jnp.mean(x * x, axis=-1, keepdims=True) + EPS)
    o_ref[...] = (y * g_ref[...]).astype(o_ref.dtype)


def _rmsnorm(x, g, out_dtype, tm=512):
    m, d = x.shape
    return pl.pallas_call(
        _rmsnorm_kernel,
        grid=(m // tm,),
        in_specs=[pl.BlockSpec((tm, d), lambda i: (i, 0)),
                  pl.BlockSpec((1, d), lambda i: (0, 0))],
        out_specs=pl.BlockSpec((tm, d), lambda i: (i, 0)),
        out_shape=jax.ShapeDtypeStruct((m, d), out_dtype),
        compiler_params=_params(1),
        name="rmsnorm",
    )(x, g.reshape(1, d))


def _matmul_kernel(a_ref, w_ref, o_ref):
    o_ref[...] = jnp.dot(a_ref[...], w_ref[...], preferred_element_type=F32).astype(o_ref.dtype)


def _proj_idx(a, w_idx, layer, tm=1024):
    m, k = a.shape
    n = w_idx.shape[2]
    tm = min(tm, m)
    assert m % tm == 0
    return pl.pallas_call(
        _matmul_kernel,
        grid=(m // tm,),
        in_specs=[pl.BlockSpec((tm, k), lambda i: (i, 0)),
                  pl.BlockSpec((None, k, n), lambda i: (layer, 0, 0))],
        out_specs=pl.BlockSpec((tm, n), lambda i: (i, 0)),
        out_shape=jax.ShapeDtypeStruct((m, n), F32),
        compiler_params=_params(1),
        name="proj_idx",
    )(a, w_idx)


def _proj_main_kernel(a_ref, wg_ref, wm_ref, o_ref, *, gate_tiles):
    j = pl.program_id(1)

    @pl.when(j < gate_tiles)
    def _():
        o_ref[...] = jnp.dot(a_ref[...], wg_ref[...], preferred_element_type=F32).astype(o_ref.dtype)

    @pl.when(j >= gate_tiles)
    def _():
        o_ref[...] = jnp.dot(a_ref[...], wm_ref[...], preferred_element_type=F32).astype(o_ref.dtype)


def _proj_main(a, w_gates, w_mix, layer, tm=1024, tn=1024):
    m, k = a.shape
    n_gates, n_mix = w_gates.shape[2], w_mix.shape[2]
    tm = min(tm, m)
    assert m % tm == 0 and n_gates % tn == 0 and n_mix % tn == 0
    gate_tiles, mix_tiles = n_gates // tn, n_mix // tn
    return pl.pallas_call(
        functools.partial(_proj_main_kernel, gate_tiles=gate_tiles),
        grid=(m // tm, gate_tiles + mix_tiles),
        in_specs=[pl.BlockSpec((tm, k), lambda i, j: (i, 0)),
                  pl.BlockSpec((None, k, tn), lambda i, j: (layer, 0, jnp.minimum(j, gate_tiles - 1))),
                  pl.BlockSpec((None, k, tn), lambda i, j: (layer, 0, jnp.maximum(j - gate_tiles, 0)))],
        out_specs=pl.BlockSpec((tm, tn), lambda i, j: (i, j)),
        out_shape=jax.ShapeDtypeStruct((m, n_gates + n_mix), BF16),
        compiler_params=_params(2),
        name="proj_main",
    )(a, w_gates, w_mix)


def _mix_a_kernel(u_ref, v_ref, z_ref, ws_ref, bt_ref, o_ref, *, n_chunks):
    ti = lax.broadcasted_iota(I32, (CHUNK, CHUNK), 0)
    si = lax.broadcasted_iota(I32, (CHUNK, CHUNK), 1)
    tril = si <= ti
    gd = A_WIDTH // A_GROUPS
    for g in range(A_GROUPS):
        w = jnp.where(tril, ws_ref[g], 0.0).astype(BF16)
        b = bt_ref[:, g:g + 1]
        for n in range(n_chunks):
            rows = slice(n * CHUNK, (n + 1) * CHUNK)
            cols = slice(g * gd, (g + 1) * gd)
            mixed = jnp.dot(w, v_ref[rows, cols], preferred_element_type=F32) + b
            u = u_ref[rows, cols].astype(F32)
            z = z_ref[rows, cols].astype(F32)
            o_ref[rows, cols] = (u * mixed * _silu(z)).astype(o_ref.dtype)


def _mix_a(p_main, ws, bias_t, sec0, tm=512):
    m = p_main.shape[0]
    col = lambda c: pl.BlockSpec((tm, SEC), lambda i, c=c: (i, sec0 + c))
    return pl.pallas_call(
        functools.partial(_mix_a_kernel, n_chunks=tm // CHUNK),
        grid=(m // tm,),
        in_specs=[col(0), col(1), col(2),
                  pl.BlockSpec(ws.shape, lambda i: (0, 0, 0)),
                  pl.BlockSpec(bias_t.shape, lambda i: (0, 0))],
        out_specs=pl.BlockSpec((tm, SEC), lambda i: (i, 0)),
        out_shape=jax.ShapeDtypeStruct((m, A_WIDTH), BF16),
        compiler_params=_params(1),
        name="mix_a",
    )(p_main, p_main, p_main, ws, bias_t)


HALO = 16


def _mix_b_kernel(bg_ref, cg_ref, x_ref, z_ref, cgh_ref, xh_ref, w_ref, o_ref, *, tiles_per_seq):
    i = pl.program_id(0)
    tm = bg_ref.shape[0]
    cx = cg_ref[...].astype(F32) * x_ref[...].astype(F32)
    halo = cgh_ref[...].astype(F32) * xh_ref[...].astype(F32)
    halo = jnp.where(i % tiles_per_seq == 0, 0.0, halo)
    row = lax.broadcasted_iota(I32, cx.shape, 0)
    cx1 = pltpu.roll(cx, 1, axis=0)
    cx1 = jnp.where(row == 0, halo[HALO - 1:HALO, :], cx1)
    cx2 = pltpu.roll(cx, 2, axis=0)
    cx2 = jnp.where(row == 0, halo[HALO - 2:HALO - 1, :], cx2)
    cx2 = jnp.where(row == 1, halo[HALO - 1:HALO, :], cx2)
    conv = w_ref[0:1, :] * cx2 + w_ref[1:2, :] * cx1 + w_ref[2:3, :] * cx
    y = bg_ref[...].astype(F32) * conv * _silu(z_ref[...].astype(F32))
    o_ref[...] = y.astype(o_ref.dtype)


def _mix_b(p_main, conv_w, sec0, seq, tm=512):
    m = p_main.shape[0]
    col = lambda c: pl.BlockSpec((tm, SEC), lambda i, c=c: (i, sec0 + c))
    halo = lambda c: pl.BlockSpec(
        (HALO, SEC), lambda i, c=c: (jnp.maximum(i * (tm // HALO) - 1, 0), sec0 + c))
    return pl.pallas_call(
        functools.partial(_mix_b_kernel, tiles_per_seq=seq // tm),
        grid=(m // tm,),
        in_specs=[col(3), col(4), col(5), col(6), halo(4), halo(5),
                  pl.BlockSpec(conv_w.shape, lambda i: (0, 0))],
        out_specs=pl.BlockSpec((tm, SEC), lambda i: (i, 0)),
        out_shape=jax.ShapeDtypeStruct((m, B_WIDTH), BF16),
        compiler_params=_params(1),
        name="mix_b",
    )(p_main, p_main, p_main, p_main, p_main, p_main, conv_w)


def _t5_thresholds():
    max_exact = NUM_BUCKETS // 2
    def bucket(n):
        v = math.log(n / max_exact) / math.log(MAX_DISTANCE / max_exact) * (NUM_BUCKETS - max_exact)
        return max_exact + int(v)
    out = []
    for k in range(max_exact + 1, NUM_BUCKETS):
        out.append(next(n for n in range(max_exact, 4 * MAX_DISTANCE) if bucket(n) >= k))
    return tuple(out)


def _compare_exchange(x, i, j):
    x[i], x[j] = jnp.maximum(x[i], x[j]), jnp.minimum(x[i], x[j])


def _bitonic_merge(x):
    x, n = list(x), len(x)
    j = n // 2
    while j >= 1:
        for i in range(n):
            if i & j == 0:
                _compare_exchange(x, i, i | j)
        j //= 2
    return x


def _bitonic_sort(x):
    x, n = list(x), len(x)
    k = 2
    while k <= n:
        j = k // 2
        while j >= 1:
            for i in range(n):
                if i & j == 0:
                    lo, hi = (i, i | j) if i & k == 0 else (i | j, i)
                    _compare_exchange(x, lo, hi)
            j //= 2
        k *= 2
    return x


def _sortable(x):
    bits = pltpu.bitcast(x, I32)
    return bits ^ ((bits >> 31) & jnp.int32(0x7FFFFFFF))


def _dsa_kernel(relb_ref, iq_ref, iwq_ref, ik_ref, q_ref, k_ref, v_ref, cz_ref, o_ref,
                key_s, mb_s, wt_s, iqh_s, bias_s, m_s, l_s, acc_s, s_s, run_s, krun_s, dmax_s, thr_s,
                *, tq, topk):
    qi = pl.program_id(1)
    n_kc = qi + 1
    kj = lax.broadcasted_iota(I32, (tq, tq), 0)
    qj = lax.broadcasted_iota(I32, (tq, tq), 1)
    log2e = math.log2(math.e)

    def causal(c):
        return (kj + (c - qi) * tq) <= qj

    @pl.when((pl.program_id(0) == 0) & (qi == 0))
    def _():
        thresholds = _t5_thresholds()
        for which in range(2):
            n = jnp.maximum(qj - kj + which * tq, 0)
            bucket = jnp.full((tq, tq), NUM_BUCKETS // 2, I32)
            for t in thresholds:
                bucket = bucket + (n >= t).astype(I32)
            bucket = jnp.where(n < NUM_BUCKETS // 2, n, bucket)
            for h in range(ATTN_HEADS):
                def fill(b, acc):
                    return jnp.where(bucket == b, relb_ref[b, h], acc)
                tile = lax.fori_loop(0, NUM_BUCKETS, fill, jnp.zeros((tq, tq), F32))
                bias_s[which, h] = (tile - relb_ref[NUM_BUCKETS - 1, h]) * log2e

    w_scale = IDX_HEADS ** -0.5 * IDX_HEAD_DIM ** -0.5
    wt_s[...] = (iwq_ref[...] * w_scale).T
    low_lanes = lax.broadcasted_iota(I32, (tq, V7X_LANES), 1) < IDX_HEAD_DIM
    heads_per_tile = V7X_LANES // IDX_HEAD_DIM
    for h in range(IDX_HEADS):
        x = iq_ref[:, (h // heads_per_tile) * V7X_LANES:(h // heads_per_tile + 1) * V7X_LANES]
        if h % heads_per_tile:
            x = pltpu.roll(x, V7X_LANES - (h % heads_per_tile) * IDX_HEAD_DIM, axis=1)
        iqh_s[h] = jnp.where(low_lanes, x, 0.0).astype(BF16)

    def score_tile(c):
        ks = pl.multiple_of(c * tq, tq)
        kc = ik_ref[pl.ds(ks, tq), :].astype(BF16)
        acc = jnp.zeros((tq, tq), F32)
        for h in range(IDX_HEADS):
            z = lax.dot_general(kc, iqh_s[h], (((1,), (1,)), ((), ())), preferred_element_type=F32)
            acc = acc + jnp.maximum(z, 0.0) * wt_s[IDX_HEAD_DIM + h:IDX_HEAD_DIM + h + 1, :]
        sc = jnp.where(causal(c), acc, NEG_INF)
        key_s[c] = _sortable(sc)
        for q_lanes in (slice(t * V7X_LANES, (t + 1) * V7X_LANES) for t in range(tq // V7X_LANES)):
            rows = _bitonic_sort([sc[SUBLANES * r:SUBLANES * (r + 1), q_lanes]
                                  for r in range(tq // SUBLANES)])
            run = [run_s[i, :, q_lanes] for i in range(RUN_DEPTH)]
            dropped = []
            for i, x in enumerate(reversed(rows)):
                j = RUN_DEPTH - len(rows) + i
                dropped.append(jnp.minimum(run[j], x))
                run[j] = jnp.maximum(run[j], x)
            run = _bitonic_merge(run)
            for i in range(RUN_DEPTH):
                run_s[i, :, q_lanes] = run[i]
            dmax_s[:, q_lanes] = functools.reduce(jnp.maximum, dropped, dmax_s[:, q_lanes])

    run_s[...] = jnp.full(run_s.shape, -jnp.inf, F32)
    dmax_s[...] = jnp.full(dmax_s.shape, -jnp.inf, F32)

    def score_tile_pair(i, carry):
        score_tile(2 * i)
        score_tile(2 * i + 1)
        return carry

    lax.fori_loop(0, n_kc // 2, score_tile_pair, 0)

    @pl.when(n_kc % 2 == 1)
    def _():
        score_tile(n_kc - 1)

    def radix_select(count_ge):
        def select_bit(i, thr):
            cand = thr + jnp.left_shift(jnp.int32(1), 31 - i)
            return jnp.where(count_ge(cand) >= topk, cand, thr)
        return lax.fori_loop(0, 32, select_bit, jnp.full((1, tq), INT_MIN, I32))

    krun_s[...] = _sortable(run_s[...])

    def count_ge_candidates(cand):
        ge = (krun_s[...] >= cand).astype(I32)
        return jnp.sum(jnp.sum(ge, axis=0), axis=0, keepdims=True)

    def count_ge_all(cand):
        def body(c, cnt):
            ge = (key_s[c] >= cand).astype(I32)
            return cnt + jnp.sum(ge.reshape(tq // SUBLANES, SUBLANES, tq), axis=0)
        cnt = lax.fori_loop(0, n_kc, body, jnp.zeros((SUBLANES, tq), I32))
        return jnp.sum(cnt, axis=0, keepdims=True)

    thr_s[...] = radix_select(count_ge_candidates)
    candidates_lossy = jnp.max((_sortable(dmax_s[...]) >= thr_s[...]).astype(I32)) > 0

    @pl.when(candidates_lossy)
    def _():
        thr_s[...] = radix_select(count_ge_all)

    thr = thr_s[...]

    def mask_tile(c, carry):
        mb_s[c] = jnp.where((key_s[c] >= thr) & causal(c), 0.0, NEG_INF)
        return carry

    lax.fori_loop(0, n_kc, mask_tile, 0)

    for h in range(ATTN_HEADS):
        m_s[h] = jnp.full((1, tq), NEG_INF, F32)
        l_s[h] = jnp.zeros((1, tq), F32)
        acc_s[h] = jnp.zeros((ATTN_HEAD_DIM, tq), F32)

    def attend(tiles):
        items = [(c, which, h) for c, which in tiles for h in range(ATTN_HEADS)]

        def logits(i):
            c, which, h = items[i]
            hs = slice(h * ATTN_HEAD_DIM, (h + 1) * ATTN_HEAD_DIM)
            kh = k_ref[pl.ds(pl.multiple_of(c * tq, tq), tq), hs]
            s = lax.dot_general(kh, q_ref[:, hs], (((1,), (1,)), ((), ())),
                                preferred_element_type=F32) + mb_s[c]
            if which is not None:
                s = s + bias_s[which, h]
            s_s[i % (HEADS_AHEAD + 1)] = s
            return jnp.max(s, axis=0, keepdims=True)

        def accumulate(i, m_cur):
            c, _, h = items[i]
            hs = slice(h * ATTN_HEAD_DIM, (h + 1) * ATTN_HEAD_DIM)
            vh = v_ref[pl.ds(pl.multiple_of(c * tq, tq), tq), hs]
            m_prev = m_s[h]
            m_new = jnp.maximum(m_prev, m_cur)
            alpha = jnp.exp2(m_prev - m_new)
            p = jnp.exp2(s_s[i % (HEADS_AHEAD + 1)] - m_new)
            l_s[h] = alpha * l_s[h] + jnp.sum(p, axis=0, keepdims=True)
            pv = lax.dot_general(vh, p.astype(BF16), (((0,), (0,)), ((), ())),
                                 preferred_element_type=F32)
            acc_s[h] = alpha * acc_s[h] + pv
            m_s[h] = m_new

        m_cur = [logits(i) for i in range(HEADS_AHEAD)]
        for i in range(len(items)):
            if i + HEADS_AHEAD < len(items):
                m_cur.append(logits(i + HEADS_AHEAD))
            accumulate(i, m_cur[i])

    n_far = jnp.maximum(qi - 1, 0)

    def attend_far_pair(i, carry):
        attend([(2 * i, None), (2 * i + 1, None)])
        return carry

    lax.fori_loop(0, n_far // 2, attend_far_pair, 0)

    @pl.when(n_far % 2 == 1)
    def _():
        attend([(n_far - 1, None)])

    @pl.when(qi >= 1)
    def _():
        attend([(qi - 1, 1), (qi, 0)])

    @pl.when(qi == 0)
    def _():
        attend([(qi, 0)])

    for h in range(ATTN_HEADS):
        hs = slice(h * ATTN_HEAD_DIM, (h + 1) * ATTN_HEAD_DIM)
        out = (acc_s[h] / l_s[h]).T
        o_ref[:, hs] = (out * _silu(cz_ref[:, hs].astype(F32))).astype(o_ref.dtype)


def _dsa(p_main, p_idx, rel_bias, sec0, batch, seq, topk, tq=256):
    m = p_main.shape[0]
    nq = seq // tq
    assert seq % tq == 0 and tq >= topk and tq // SUBLANES <= RUN_DEPTH
    assert RUN_DEPTH & (RUN_DEPTH - 1) == 0 and (tq // SUBLANES) & (tq // SUBLANES - 1) == 0
    idx_cols = IDX_HEADS * IDX_HEAD_DIM // V7X_LANES
    row = lambda b, q: b * nq + q
    kernel = functools.partial(_dsa_kernel, tq=tq, topk=topk)
    return pl.pallas_call(
        kernel,
        grid=(batch, nq),
        in_specs=[
            pl.BlockSpec(memory_space=pltpu.SMEM),
            pl.BlockSpec((tq, IDX_HEADS * IDX_HEAD_DIM), lambda b, q: (row(b, q), 0)),
            pl.BlockSpec((tq, V7X_LANES), lambda b, q: (row(b, q), idx_cols)),
            _resident((seq, V7X_LANES), lambda b, q: (b, idx_cols)),
            pl.BlockSpec((tq, SEC), lambda b, q: (row(b, q), sec0 + 7)),
            _resident((seq, SEC), lambda b, q: (b, sec0 + 8)),
            _resident((seq, SEC), lambda b, q: (b, sec0 + 9)),
            pl.BlockSpec((tq, SEC), lambda b, q: (row(b, q), sec0 + 10)),
        ],
        out_specs=pl.BlockSpec((tq, SEC), lambda b, q: (row(b, q), 0)),
        out_shape=jax.ShapeDtypeStruct((m, C_WIDTH), BF16),
        scratch_shapes=[
            pltpu.VMEM((nq, tq, tq), I32),
            pltpu.VMEM((nq, tq, tq), F32),
            pltpu.VMEM((V7X_LANES, tq), F32),
            pltpu.VMEM((IDX_HEADS, tq, V7X_LANES), BF16),
            pltpu.VMEM((2, ATTN_HEADS, tq, tq), F32),
            pltpu.VMEM((ATTN_HEADS, 1, tq), F32),
            pltpu.VMEM((ATTN_HEADS, 1, tq), F32),
            pltpu.VMEM((ATTN_HEADS, ATTN_HEAD_DIM, tq), F32),
            pltpu.VMEM((HEADS_AHEAD + 1, tq, tq), F32),
            pltpu.VMEM((RUN_DEPTH, SUBLANES, tq), F32),
            pltpu.VMEM((RUN_DEPTH, SUBLANES, tq), I32),
            pltpu.VMEM((SUBLANES, tq), F32),
            pltpu.VMEM((1, tq), I32),
        ],
        compiler_params=_params(2),
        name="dsa",
    )(rel_bias, p_idx, p_idx, p_idx, p_main, p_main, p_main, p_main)


def _merge_kernel(ya_ref, yb_ref, yc_ref, ga_ref, gb_ref, gc_ref, x_ref,
                  pa_ref, pb_ref, pc_ref, wo_ref, o_ref):
    def branch(y_ref, g_ref, p_ref):
        return _sigmoid(g_ref[...].astype(F32)) * jnp.dot(y_ref[...], p_ref[...], preferred_element_type=F32)
    mix = branch(ya_ref, ga_ref, pa_ref) + branch(yb_ref, gb_ref, pb_ref) + branch(yc_ref, gc_ref, pc_ref)
    o_ref[...] = x_ref[...] + jnp.dot(mix.astype(BF16), wo_ref[...], preferred_element_type=F32)


def _merge(y_a, y_b, y_c, p_main, x, p_a, p_b, p_c, w_o, layer, tm=256):
    m, d = x.shape
    ytile = pl.BlockSpec((tm, SEC), lambda i: (i, 0))
    gate = lambda c: pl.BlockSpec((tm, d), lambda i, c=c: (i, c))
    return pl.pallas_call(
        _merge_kernel,
        grid=(m // tm,),
        in_specs=[ytile, ytile, ytile, gate(0), gate(1), gate(2),
                  pl.BlockSpec((tm, d), lambda i: (i, 0)),
                  _resident((None, SEC, d), lambda i: (layer, 0, 0)),
                  _resident((None, SEC, d), lambda i: (layer, 0, 0)),
                  _resident((None, SEC, d), lambda i: (layer, 0, 0)),
                  _resident((None, d, d), lambda i: (layer, 0, 0))],
        out_specs=pl.BlockSpec((tm, d), lambda i: (i, 0)),
        out_shape=jax.ShapeDtypeStruct((m, d), F32),
        compiler_params=_params(1),
        name="merge",
    )(y_a, y_b, y_c, p_main, p_main, p_main, x, p_a, p_b, p_c, w_o)


def kernel(x, norm_g, w_in, a_ws, a_b, b_conv, p_a, p_b, p_c, w_o, rel_bias, final_g):
    batch, seq, d = x.shape
    depth = w_in.shape[0]
    m = batch * seq
    topk = min(TOPK_MAX, seq // 4)
    n_mix = 11 * SEC
    n_idx = IDX_HEADS * IDX_HEAD_DIM + IDX_HEAD_DIM + IDX_HEADS
    assert w_in.shape[2] == n_mix + n_idx + 3 * d and (3 * d) % SEC == 0
    sec0 = 3 * d // SEC
    idx_pad = -n_idx % V7X_LANES

    q_scale = ATTN_HEAD_DIM ** -0.5 * math.log2(math.e)
    col = jnp.arange(n_mix)
    col_scale = jnp.where((col >= 7 * SEC) & (col < 8 * SEC), q_scale, 1.0).astype(F32)
    w_gates = w_in[:, :, n_mix + n_idx:].astype(BF16)
    w_mix = (w_in[:, :, :n_mix] * col_scale).astype(BF16)
    w_idx = jnp.pad(w_in[:, :, n_mix:n_mix + n_idx], ((0, 0), (0, 0), (0, idx_pad))).astype(BF16)
    a_bt = jnp.swapaxes(a_b, 1, 2)
    p_a16, p_b16, p_c16, w_o16 = (w.astype(BF16) for w in (p_a, p_b, p_c, w_o))

    xf = x.reshape(m, d)
    for l in range(depth):
        h = _rmsnorm(xf, norm_g[l], BF16)
        p_main = _proj_main(h, w_gates, w_mix, l)
        p_idx = _proj_idx(h, w_idx, l)
        y_a = _mix_a(p_main, a_ws[l], a_bt[l], sec0)
        y_b = _mix_b(p_main, b_conv[l], sec0, seq)
        y_c = _dsa(p_main, p_idx, rel_bias, sec0, batch, seq, topk)
        xf = _merge(y_a, y_b, y_c, p_main, xf, p_a16, p_b16, p_c16, w_o16, l)
    return _rmsnorm(xf, final_g, x.dtype).reshape(batch, seq, d)
```

```python
import functools
import math

import jax
import jax.numpy as jnp
from jax import lax
from jax.experimental import pallas as pl
from jax.experimental.pallas import tpu as pltpu

EPS = 1e-6
CHUNK = 128
A_GROUPS = 8
A_WIDTH = 1024
B_WIDTH = 1024
SHORT_CONV = 3
ATTN_HEADS = 8
ATTN_HEAD_DIM = 128
C_WIDTH = ATTN_HEADS * ATTN_HEAD_DIM
IDX_HEADS = 16
IDX_HEAD_DIM = 64
TOPK_MAX = 256
NUM_BUCKETS = 32
MAX_DISTANCE = 128
NEG_INF = -1e30
SEC = 1024
RUN_DEPTH = 64
HEADS_AHEAD = 3

V7X_LANES = 128
SUBLANES = 8
V7X_VMEM_BYTES = 64 * 1024 * 1024
VMEM_LIMIT = V7X_VMEM_BYTES - 8 * 1024 * 1024

F32 = jnp.float32
BF16 = jnp.bfloat16
I32 = jnp.int32
INT_MIN = -(2 ** 31)


def _params(n_axes):
    return pltpu.CompilerParams(
        dimension_semantics=("arbitrary",) * n_axes, vmem_limit_bytes=VMEM_LIMIT)


def _sigmoid(x):
    return 1.0 / (1.0 + jnp.exp(-x))


def _silu(x):
    return x * _sigmoid(x)


def _resident(block_shape, index_map):
    return pl.BlockSpec(block_shape, index_map, pipeline_mode=pl.Buffered(1))


def _rmsnorm_kernel(x_ref, g_ref, o_ref):
    x = x_ref[...]
    y = x * lax.rsqrt(jnp.mean(x * x, axis=-1, keepdims=True) + EPS)
    o_ref[...] = (y * g_ref[...]).astype(o_ref.dtype)


def _rmsnorm(x, g, out_dtype, tm=512):
    m, d = x.shape
    return pl.pallas_call(
        _rmsnorm_kernel,
        grid=(m // tm,),
        in_specs=[pl.BlockSpec((tm, d), lambda i: (i, 0)),
                  pl.BlockSpec((1, d), lambda i: (0, 0))],
        out_specs=pl.BlockSpec((tm, d), lambda i: (i, 0)),
        out_shape=jax.ShapeDtypeStruct((m, d), out_dtype),
        compiler_params=_params(1),
        name="rmsnorm",
    )(x, g.reshape(1, d))


def _cast_tail_kernel(a_ref, b_ref, o_ref, *, lane_offset):
    x = jnp.concatenate([a_ref[...], b_ref[...]], axis=1)
    o_ref[...] = x[:, lane_offset:lane_offset + o_ref.shape[1]].astype(o_ref.dtype)


def _cast_tail_columns(w, start, tr=512, tc=1024):
    depth, rows, cols = w.shape
    n = cols - start
    base = start // V7X_LANES * V7X_LANES
    tr = min(tr, rows)
    assert rows % tr == 0 and n % tc == 0 and base % tc == 0
    return pl.pallas_call(
        functools.partial(_cast_tail_kernel, lane_offset=start - base),
        grid=(depth, rows // tr, n // tc),
        in_specs=[pl.BlockSpec((None, tr, tc), lambda l, r, j: (l, r, base // tc + j)),
                  pl.BlockSpec((None, tr, V7X_LANES),
                               lambda l, r, j: (l, r, (base + tc * (j + 1)) // V7X_LANES))],
        out_specs=pl.BlockSpec((None, tr, tc), lambda l, r, j: (l, r, j)),
        out_shape=jax.ShapeDtypeStruct((depth, rows, n), BF16),
        compiler_params=_params(3),
        name="cast_tail",
    )(w, w)


def _matmul_kernel(a_ref, w_ref, o_ref):
    o_ref[...] = jnp.dot(a_ref[...], w_ref[...], preferred_element_type=F32).astype(o_ref.dtype)


def _proj_idx(a, w_idx, layer, tm=1024):
    m, k = a.shape
    n = w_idx.shape[2]
    tm = min(tm, m)
    assert m % tm == 0
    return pl.pallas_call(
        _matmul_kernel,
        grid=(m // tm,),
        in_specs=[pl.BlockSpec((tm, k), lambda i: (i, 0)),
                  pl.BlockSpec((None, k, n), lambda i: (layer, 0, 0))],
        out_specs=pl.BlockSpec((tm, n), lambda i: (i, 0)),
        out_shape=jax.ShapeDtypeStruct((m, n), F32),
        compiler_params=_params(1),
        name="proj_idx",
    )(a, w_idx)


def _proj_main_kernel(a_ref, wg_ref, wm_ref, o_ref, *, gate_tiles):
    j = pl.program_id(1)

    @pl.when(j < gate_tiles)
    def _():
        o_ref[...] = jnp.dot(a_ref[...], wg_ref[...], preferred_element_type=F32).astype(o_ref.dtype)

    @pl.when(j >= gate_tiles)
    def _():
        o_ref[...] = jnp.dot(a_ref[...], wm_ref[...], preferred_element_type=F32).astype(o_ref.dtype)


def _proj_main(a, w_gates, w_mix, layer, tm=1024, tn=1024):
    m, k = a.shape
    n_gates, n_mix = w_gates.shape[2], w_mix.shape[2]
    tm = min(tm, m)
    assert m % tm == 0 and n_gates % tn == 0 and n_mix % tn == 0
    gate_tiles, mix_tiles = n_gates // tn, n_mix // tn
    return pl.pallas_call(
        functools.partial(_proj_main_kernel, gate_tiles=gate_tiles),
        grid=(m // tm, gate_tiles + mix_tiles),
        in_specs=[pl.BlockSpec((tm, k), lambda i, j: (i, 0)),
                  pl.BlockSpec((None, k, tn), lambda i, j: (layer, 0, jnp.minimum(j, gate_tiles - 1))),
                  pl.BlockSpec((None, k, tn), lambda i, j: (layer, 0, jnp.maximum(j - gate_tiles, 0)))],
        out_specs=pl.BlockSpec((tm, tn), lambda i, j: (i, j)),
        out_shape=jax.ShapeDtypeStruct((m, n_gates + n_mix), BF16),
        compiler_params=_params(2),
        name="proj_main",
    )(a, w_gates, w_mix)


def _mix_a_kernel(u_ref, v_ref, z_ref, ws_ref, bt_ref, o_ref, *, n_chunks):
    ti = lax.broadcasted_iota(I32, (CHUNK, CHUNK), 0)
    si = lax.broadcasted_iota(I32, (CHUNK, CHUNK), 1)
    tril = si <= ti
    gd = A_WIDTH // A_GROUPS
    for g in range(A_GROUPS):
        w = jnp.where(tril, ws_ref[g], 0.0).astype(BF16)
        b = bt_ref[:, g:g + 1]
        for n in range(n_chunks):
            rows = slice(n * CHUNK, (n + 1) * CHUNK)
            cols = slice(g * gd, (g + 1) * gd)
            mixed = jnp.dot(w, v_ref[rows, cols], preferred_element_type=F32) + b
            u = u_ref[rows, cols].astype(F32)
            z = z_ref[rows, cols].astype(F32)
            o_ref[rows, cols] = (u * mixed * _silu(z)).astype(o_ref.dtype)


def _mix_a(p_main, ws, bias_t, sec0, tm=512):
    m = p_main.shape[0]
    col = lambda c: pl.BlockSpec((tm, SEC), lambda i, c=c: (i, sec0 + c))
    return pl.pallas_call(
        functools.partial(_mix_a_kernel, n_chunks=tm // CHUNK),
        grid=(m // tm,),
        in_specs=[col(0), col(1), col(2),
                  pl.BlockSpec(ws.shape, lambda i: (0, 0, 0)),
                  pl.BlockSpec(bias_t.shape, lambda i: (0, 0))],
        out_specs=pl.BlockSpec((tm, SEC), lambda i: (i, 0)),
        out_shape=jax.ShapeDtypeStruct((m, A_WIDTH), BF16),
        compiler_params=_params(1),
        name="mix_a",
    )(p_main, p_main, p_main, ws, bias_t)


HALO = 16


def _mix_b_kernel(bg_ref, cg_ref, x_ref, z_ref, cgh_ref, xh_ref, w_ref, o_ref, *, tiles_per_seq):
    i = pl.program_id(0)
    tm = bg_ref.shape[0]
    cx = cg_ref[...].astype(F32) * x_ref[...].astype(F32)
    halo = cgh_ref[...].astype(F32) * xh_ref[...].astype(F32)
    halo = jnp.where(i % tiles_per_seq == 0, 0.0, halo)
    row = lax.broadcasted_iota(I32, cx.shape, 0)
    cx1 = pltpu.roll(cx, 1, axis=0)
    cx1 = jnp.where(row == 0, halo[HALO - 1:HALO, :], cx1)
    cx2 = pltpu.roll(cx, 2, axis=0)
    cx2 = jnp.where(row == 0, halo[HALO - 2:HALO - 1, :], cx2)
    cx2 = jnp.where(row == 1, halo[HALO - 1:HALO, :], cx2)
    conv = w_ref[0:1, :] * cx2 + w_ref[1:2, :] * cx1 + w_ref[2:3, :] * cx
    y = bg_ref[...].astype(F32) * conv * _silu(z_ref[...].astype(F32))
    o_ref[...] = y.astype(o_ref.dtype)


def _mix_b(p_main, conv_w, sec0, seq, tm=512):
    m = p_main.shape[0]
    col = lambda c: pl.BlockSpec((tm, SEC), lambda i, c=c: (i, sec0 + c))
    halo = lambda c: pl.BlockSpec(
        (HALO, SEC), lambda i, c=c: (jnp.maximum(i * (tm // HALO) - 1, 0), sec0 + c))
    return pl.pallas_call(
        functools.partial(_mix_b_kernel, tiles_per_seq=seq // tm),
        grid=(m // tm,),
        in_specs=[col(3), col(4), col(5), col(6), halo(4), halo(5),
                  pl.BlockSpec(conv_w.shape, lambda i: (0, 0))],
        out_specs=pl.BlockSpec((tm, SEC), lambda i: (i, 0)),
        out_shape=jax.ShapeDtypeStruct((m, B_WIDTH), BF16),
        compiler_params=_params(1),
        name="mix_b",
    )(p_main, p_main, p_main, p_main, p_main, p_main, conv_w)


def _t5_thresholds():
    max_exact = NUM_BUCKETS // 2
    def bucket(n):
        v = math.log(n / max_exact) / math.log(MAX_DISTANCE / max_exact) * (NUM_BUCKETS - max_exact)
        return max_exact + int(v)
    out = []
    for k in range(max_exact + 1, NUM_BUCKETS):
        out.append(next(n for n in range(max_exact, 4 * MAX_DISTANCE) if bucket(n) >= k))
    return tuple(out)


def _compare_exchange(x, i, j):
    x[i], x[j] = jnp.maximum(x[i], x[j]), jnp.minimum(x[i], x[j])


def _bitonic_merge(x):
    x, n = list(x), len(x)
    j = n // 2
    while j >= 1:
        for i in range(n):
            if i & j == 0:
                _compare_exchange(x, i, i | j)
        j //= 2
    return x


def _bitonic_sort(x):
    x, n = list(x), len(x)
    k = 2
    while k <= n:
        j = k // 2
        while j >= 1:
            for i in range(n):
                if i & j == 0:
                    lo, hi = (i, i | j) if i & k == 0 else (i | j, i)
                    _compare_exchange(x, lo, hi)
            j //= 2
        k *= 2
    return x


def _sortable(x):
    bits = pltpu.bitcast(x, I32)
    return bits ^ ((bits >> 31) & jnp.int32(0x7FFFFFFF))


def _dsa_kernel(relb_ref, iq_ref, iwq_ref, ik_ref, q_ref, k_ref, v_ref, cz_ref, o_ref,
                key_s, mb_s, wt_s, iqh_s, bias_s, m_s, l_s, acc_s, s_s, run_s, krun_s, dmax_s, thr_s, nge_s,
                *, tq, topk):
    qi = pl.program_id(1)
    n_kc = qi + 1
    kj = lax.broadcasted_iota(I32, (tq, tq), 0)
    qj = lax.broadcasted_iota(I32, (tq, tq), 1)
    log2e = math.log2(math.e)

    def causal(c):
        return (kj + (c - qi) * tq) <= qj

    @pl.when((pl.program_id(0) == 0) & (qi == 0))
    def _():
        thresholds = _t5_thresholds()
        for which in range(2):
            n = jnp.maximum(qj - kj + which * tq, 0)
            bucket = jnp.full((tq, tq), NUM_BUCKETS // 2, I32)
            for t in thresholds:
                bucket = bucket + (n >= t).astype(I32)
            bucket = jnp.where(n < NUM_BUCKETS // 2, n, bucket)
            for h in range(ATTN_HEADS):
                def fill(b, acc):
                    return jnp.where(bucket == b, relb_ref[b, h], acc)
                tile = lax.fori_loop(0, NUM_BUCKETS, fill, jnp.zeros((tq, tq), F32))
                bias_s[which, h] = (tile - relb_ref[NUM_BUCKETS - 1, h]) * log2e

    w_scale = IDX_HEADS ** -0.5 * IDX_HEAD_DIM ** -0.5
    wt_s[...] = (iwq_ref[...] * w_scale).T
    low_lanes = lax.broadcasted_iota(I32, (tq, V7X_LANES), 1) < IDX_HEAD_DIM
    heads_per_tile = V7X_LANES // IDX_HEAD_DIM
    for h in range(IDX_HEADS):
        x = iq_ref[:, (h // heads_per_tile) * V7X_LANES:(h // heads_per_tile + 1) * V7X_LANES]
        if h % heads_per_tile:
            x = pltpu.roll(x, V7X_LANES - (h % heads_per_tile) * IDX_HEAD_DIM, axis=1)
        iqh_s[h] = jnp.where(low_lanes, x, 0.0).astype(BF16)

    def score_tile(c):
        ks = pl.multiple_of(c * tq, tq)
        kc = ik_ref[pl.ds(ks, tq), :].astype(BF16)
        acc = jnp.zeros((tq, tq), F32)
        for h in range(IDX_HEADS):
            z = lax.dot_general(kc, iqh_s[h], (((1,), (1,)), ((), ())), preferred_element_type=F32)
            acc = acc + jnp.maximum(z, 0.0) * wt_s[IDX_HEAD_DIM + h:IDX_HEAD_DIM + h + 1, :]
        sc = jnp.where(causal(c), acc, NEG_INF)
        key_s[c] = _sortable(sc)
        for q_lanes in (slice(t * V7X_LANES, (t + 1) * V7X_LANES) for t in range(tq // V7X_LANES)):
            rows = _bitonic_sort([sc[SUBLANES * r:SUBLANES * (r + 1), q_lanes]
                                  for r in range(tq // SUBLANES)])
            run = [run_s[i, :, q_lanes] for i in range(RUN_DEPTH)]
            dropped = []
            for i, x in enumerate(reversed(rows)):
                j = RUN_DEPTH - len(rows) + i
                dropped.append(jnp.minimum(run[j], x))
                run[j] = jnp.maximum(run[j], x)
            run = _bitonic_merge(run)
            for i in range(RUN_DEPTH):
                run_s[i, :, q_lanes] = run[i]
            dmax_s[:, q_lanes] = functools.reduce(jnp.maximum, dropped, dmax_s[:, q_lanes])

    run_s[...] = jnp.full(run_s.shape, -jnp.inf, F32)
    dmax_s[...] = jnp.full(dmax_s.shape, -jnp.inf, F32)

    def score_tile_pair(i, carry):
        score_tile(2 * i)
        score_tile(2 * i + 1)
        return carry

    lax.fori_loop(0, n_kc // 2, score_tile_pair, 0)

    @pl.when(n_kc % 2 == 1)
    def _():
        score_tile(n_kc - 1)

    def radix_select(count_ge):
        def select_bit(i, thr):
            cand = thr + jnp.left_shift(jnp.int32(1), 31 - i)
            return jnp.where(count_ge(cand) >= topk, cand, thr)
        return lax.fori_loop(0, 32, select_bit, jnp.full((1, tq), INT_MIN, I32))

    krun_s[...] = _sortable(run_s[...])

    def count_ge_candidates(cand):
        ge = (krun_s[...] >= cand).astype(I32)
        return jnp.sum(jnp.sum(ge, axis=0), axis=0, keepdims=True)

    def count_ge_all(cand):
        def body(c, cnt):
            ge = (key_s[c] >= cand).astype(I32)
            return cnt + jnp.sum(ge.reshape(tq // SUBLANES, SUBLANES, tq), axis=0)
        cnt = lax.fori_loop(0, n_kc, body, jnp.zeros((SUBLANES, tq), I32))
        return jnp.sum(cnt, axis=0, keepdims=True)

    thr_s[...] = radix_select(count_ge_candidates)
    nge_s[...] = count_ge_candidates(thr_s[...])
    candidates_lossy = jnp.max((_sortable(dmax_s[...]) >= thr_s[...]).astype(I32)) > 0

    @pl.when(candidates_lossy)
    def _():
        thr_s[...] = radix_select(count_ge_all)
        nge_s[...] = count_ge_all(thr_s[...])

    thr = thr_s[...]

    def mask_tile(c, carry):
        mb_s[c] = jnp.where((key_s[c] >= thr) & causal(c), 0.0, NEG_INF)
        return carry

    lax.fori_loop(0, n_kc, mask_tile, 0)

    @pl.when(jnp.max(nge_s[...]) > topk)
    def _():
        need = (topk - count_ge_all(thr + 1)).astype(F32)
        prefix = (qj <= kj).astype(BF16)

        def mask_tile_ranked(c, seen):
            key = key_s[c]
            tied = key == thr
            tied_f = jnp.where(tied, 1.0, 0.0)
            rank = seen + jnp.dot(prefix, tied_f.astype(BF16), preferred_element_type=F32)
            keep = (key > thr) | (tied & (rank <= need))
            mb_s[c] = jnp.where(keep & causal(c), 0.0, NEG_INF)
            return seen + jnp.sum(tied_f, axis=0, keepdims=True)

        lax.fori_loop(0, n_kc, mask_tile_ranked, jnp.zeros((1, tq), F32))

    for h in range(ATTN_HEADS):
        m_s[h] = jnp.full((1, tq), NEG_INF, F32)
        l_s[h] = jnp.zeros((1, tq), F32)
        acc_s[h] = jnp.zeros((ATTN_HEAD_DIM, tq), F32)

    def attend(tiles):
        items = [(c, which, h) for c, which in tiles for h in range(ATTN_HEADS)]

        def logits(i):
            c, which, h = items[i]
            hs = slice(h * ATTN_HEAD_DIM, (h + 1) * ATTN_HEAD_DIM)
            kh = k_ref[pl.ds(pl.multiple_of(c * tq, tq), tq), hs]
            s = lax.dot_general(kh, q_ref[:, hs], (((1,), (1,)), ((), ())),
                                preferred_element_type=F32) + mb_s[c]
            if which is not None:
                s = s + bias_s[which, h]
            s_s[i % (HEADS_AHEAD + 1)] = s
            return jnp.max(s, axis=0, keepdims=True)

        def accumulate(i, m_cur):
            c, _, h = items[i]
            hs = slice(h * ATTN_HEAD_DIM, (h + 1) * ATTN_HEAD_DIM)
            vh = v_ref[pl.ds(pl.multiple_of(c * tq, tq), tq), hs]
            m_prev = m_s[h]
            m_new = jnp.maximum(m_prev, m_cur)
            alpha = jnp.exp2(m_prev - m_new)
            p = jnp.exp2(s_s[i % (HEADS_AHEAD + 1)] - m_new)
            l_s[h] = alpha * l_s[h] + jnp.sum(p, axis=0, keepdims=True)
            pv = lax.dot_general(vh, p.astype(BF16), (((0,), (0,)), ((), ())),
                                 preferred_element_type=F32)
            acc_s[h] = alpha * acc_s[h] + pv
            m_s[h] = m_new

        m_cur = [logits(i) for i in range(HEADS_AHEAD)]
        for i in range(len(items)):
            if i + HEADS_AHEAD < len(items):
                m_cur.append(logits(i + HEADS_AHEAD))
            accumulate(i, m_cur[i])

    n_far = jnp.maximum(qi - 1, 0)

    def attend_far_pair(i, carry):
        attend([(2 * i, None), (2 * i + 1, None)])
        return carry

    lax.fori_loop(0, n_far // 2, attend_far_pair, 0)

    @pl.when(n_far % 2 == 1)
    def _():
        attend([(n_far - 1, None)])

    @pl.when(qi >= 1)
    def _():
        attend([(qi - 1, 1), (qi, 0)])

    @pl.when(qi == 0)
    def _():
        attend([(qi, 0)])

    for h in range(ATTN_HEADS):
        hs = slice(h * ATTN_HEAD_DIM, (h + 1) * ATTN_HEAD_DIM)
        out = (acc_s[h] / l_s[h]).T
        o_ref[:, hs] = (out * _silu(cz_ref[:, hs].astype(F32))).astype(o_ref.dtype)


def _dsa(p_main, p_idx, rel_bias, sec0, batch, seq, topk, tq=256):
    m = p_main.shape[0]
    nq = seq // tq
    assert seq % tq == 0 and tq >= topk and tq // SUBLANES <= RUN_DEPTH
    assert RUN_DEPTH & (RUN_DEPTH - 1) == 0 and (tq // SUBLANES) & (tq // SUBLANES - 1) == 0
    idx_cols = IDX_HEADS * IDX_HEAD_DIM // V7X_LANES
    row = lambda b, q: b * nq + q
    kernel = functools.partial(_dsa_kernel, tq=tq, topk=topk)
    return pl.pallas_call(
        kernel,
        grid=(batch, nq),
        in_specs=[
            pl.BlockSpec(memory_space=pltpu.SMEM),
            pl.BlockSpec((tq, IDX_HEADS * IDX_HEAD_DIM), lambda b, q: (row(b, q), 0)),
            pl.BlockSpec((tq, V7X_LANES), lambda b, q: (row(b, q), idx_cols)),
            _resident((seq, V7X_LANES), lambda b, q: (b, idx_cols)),
            pl.BlockSpec((tq, SEC), lambda b, q: (row(b, q), sec0 + 7)),
            _resident((seq, SEC), lambda b, q: (b, sec0 + 8)),
            _resident((seq, SEC), lambda b, q: (b, sec0 + 9)),
            pl.BlockSpec((tq, SEC), lambda b, q: (row(b, q), sec0 + 10)),
        ],
        out_specs=pl.BlockSpec((tq, SEC), lambda b, q: (row(b, q), 0)),
        out_shape=jax.ShapeDtypeStruct((m, C_WIDTH), BF16),
        scratch_shapes=[
            pltpu.VMEM((nq, tq, tq), I32),
            pltpu.VMEM((nq, tq, tq), F32),
            pltpu.VMEM((V7X_LANES, tq), F32),
            pltpu.VMEM((IDX_HEADS, tq, V7X_LANES), BF16),
            pltpu.VMEM((2, ATTN_HEADS, tq, tq), F32),
            pltpu.VMEM((ATTN_HEADS, 1, tq), F32),
            pltpu.VMEM((ATTN_HEADS, 1, tq), F32),
            pltpu.VMEM((ATTN_HEADS, ATTN_HEAD_DIM, tq), F32),
            pltpu.VMEM((HEADS_AHEAD + 1, tq, tq), F32),
            pltpu.VMEM((RUN_DEPTH, SUBLANES, tq), F32),
            pltpu.VMEM((RUN_DEPTH, SUBLANES, tq), I32),
            pltpu.VMEM((SUBLANES, tq), F32),
            pltpu.VMEM((1, tq), I32),
            pltpu.VMEM((1, tq), I32),
        ],
        compiler_params=_params(2),
        name="dsa",
    )(rel_bias, p_idx, p_idx, p_idx, p_main, p_main, p_main, p_main)


def _merge_kernel(ya_ref, yb_ref, yc_ref, ga_ref, gb_ref, gc_ref, x_ref,
                  pa_ref, pb_ref, pc_ref, wo_ref, o_ref):
    def branch(y_ref, g_ref, p_ref):
        return _sigmoid(g_ref[...].astype(F32)) * jnp.dot(y_ref[...], p_ref[...], preferred_element_type=F32)
    mix = branch(ya_ref, ga_ref, pa_ref) + branch(yb_ref, gb_ref, pb_ref) + branch(yc_ref, gc_ref, pc_ref)
    o_ref[...] = x_ref[...] + jnp.dot(mix.astype(BF16), wo_ref[...], preferred_element_type=F32)


def _merge(y_a, y_b, y_c, p_main, x, p_a, p_b, p_c, w_o, layer, tm=256):
    m, d = x.shape
    ytile = pl.BlockSpec((tm, SEC), lambda i: (i, 0))
    gate = lambda c: pl.BlockSpec((tm, d), lambda i, c=c: (i, c))
    return pl.pallas_call(
        _merge_kernel,
        grid=(m // tm,),
        in_specs=[ytile, ytile, ytile, gate(0), gate(1), gate(2),
                  pl.BlockSpec((tm, d), lambda i: (i, 0)),
                  _resident((None, SEC, d), lambda i: (layer, 0, 0)),
                  _resident((None, SEC, d), lambda i: (layer, 0, 0)),
                  _resident((None, SEC, d), lambda i: (layer, 0, 0)),
                  _resident((None, d, d), lambda i: (layer, 0, 0))],
        out_specs=pl.BlockSpec((tm, d), lambda i: (i, 0)),
        out_shape=jax.ShapeDtypeStruct((m, d), F32),
        compiler_params=_params(1),
        name="merge",
    )(y_a, y_b, y_c, p_main, p_main, p_main, x, p_a, p_b, p_c, w_o)


def kernel(x, norm_g, w_in, a_ws, a_b, b_conv, p_a, p_b, p_c, w_o, rel_bias, final_g):
    batch, seq, d = x.shape
    depth = w_in.shape[0]
    m = batch * seq
    topk = min(TOPK_MAX, seq // 4)
    n_mix = 11 * SEC
    n_idx = IDX_HEADS * IDX_HEAD_DIM + IDX_HEAD_DIM + IDX_HEADS
    assert w_in.shape[2] == n_mix + n_idx + 3 * d and (3 * d) % SEC == 0
    sec0 = 3 * d // SEC
    idx_pad = -n_idx % V7X_LANES

    q_scale = ATTN_HEAD_DIM ** -0.5 * math.log2(math.e)
    col = jnp.arange(n_mix)
    col_scale = jnp.where((col >= 7 * SEC) & (col < 8 * SEC), q_scale, 1.0).astype(F32)
    w_gates = _cast_tail_columns(w_in, n_mix + n_idx)
    w_mix = (w_in[:, :, :n_mix] * col_scale).astype(BF16)
    w_idx = jnp.pad(w_in[:, :, n_mix:n_mix + n_idx], ((0, 0), (0, 0), (0, idx_pad))).astype(BF16)
    a_bt = jnp.swapaxes(a_b, 1, 2)
    p_a16, p_b16, p_c16, w_o16 = (w.astype(BF16) for w in (p_a, p_b, p_c, w_o))

    xf = x.reshape(m, d)
    for l in range(depth):
        h = _rmsnorm(xf, norm_g[l], BF16)
        p_main = _proj_main(h, w_gates, w_mix, l)
        p_idx = _proj_idx(h, w_idx, l)
        y_a = _mix_a(p_main, a_ws[l], a_bt[l], sec0)
        y_b = _mix_b(p_main, b_conv[l], sec0, seq)
        y_c = _dsa(p_main, p_idx, rel_bias, sec0, batch, seq, topk)
        xf = _merge(y_a, y_b, y_c, p_main, xf, p_a16, p_b16, p_c16, w_o16, l)
    return _rmsnorm(xf, final_g, x.dtype).reshape(batch, seq, d)
```

```python
import functools
import math

import jax
import jax.numpy as jnp
from jax import lax
from jax.experimental import pallas as pl
from jax.experimental.pallas import tpu as pltpu

EPS = 1e-6
CHUNK = 128
A_GROUPS = 8
A_WIDTH = 1024
B_WIDTH = 1024
SHORT_CONV = 3
ATTN_HEADS = 8
ATTN_HEAD_DIM = 128
C_WIDTH = ATTN_HEADS * ATTN_HEAD_DIM
IDX_HEADS = 16
IDX_HEAD_DIM = 64
TOPK_MAX = 256
NUM_BUCKETS = 32
MAX_DISTANCE = 128
NEG_INF = -1e30
SEC = 1024
RUN_DEPTH = 64
HEADS_AHEAD = 3

V7X_LANES = 128
SUBLANES = 8
BF16_SUBLANES = 16
CAST_HALO = 128
V7X_VMEM_BYTES = 64 * 1024 * 1024
VMEM_LIMIT = V7X_VMEM_BYTES - 8 * 1024 * 1024

F32 = jnp.float32
BF16 = jnp.bfloat16
I32 = jnp.int32
INT_MIN = -(2 ** 31)


def _params(n_axes):
    return pltpu.CompilerParams(
        dimension_semantics=("arbitrary",) * n_axes, vmem_limit_bytes=VMEM_LIMIT)


def _sigmoid(x):
    return 1.0 / (1.0 + jnp.exp(-x))


def _silu(x):
    return x * _sigmoid(x)


def _resident(block_shape, index_map):
    return pl.BlockSpec(block_shape, index_map, pipeline_mode=pl.Buffered(1))


def _rmsnorm_rows(x, g):
    return x * lax.rsqrt(jnp.mean(x * x, axis=-1, keepdims=True) + EPS) * g


def _rmsnorm_kernel(x_ref, g_ref, o_ref):
    o_ref[...] = _rmsnorm_rows(x_ref[...], g_ref[...]).astype(o_ref.dtype)


def _rmsnorm(x, g, out_dtype, tm=512):
    m, d = x.shape
    return pl.pallas_call(
        _rmsnorm_kernel,
        grid=(m // tm,),
        in_specs=[pl.BlockSpec((tm, d), lambda i: (i, 0)),
                  pl.BlockSpec((1, d), lambda i: (0, 0))],
        out_specs=pl.BlockSpec((tm, d), lambda i: (i, 0)),
        out_shape=jax.ShapeDtypeStruct((m, d), out_dtype),
        compiler_params=_params(1),
        name="rmsnorm",
    )(x, g.reshape(1, d))


def _cast_rows_kernel(*refs, row_offset, valid, scaled_tiles, scale):
    o_ref = refs[-1]
    tr = o_ref.shape[0]
    j = pl.program_id(1)
    x = refs[0][...]
    if row_offset:
        x = jnp.concatenate([x, refs[1][...]], axis=0)[row_offset:row_offset + tr, :]
    if scaled_tiles is not None:
        x = x * jnp.where((j >= scaled_tiles[0]) & (j < scaled_tiles[1]), scale, 1.0)
    if valid % tr:
        row = lax.broadcasted_iota(I32, x.shape, 0)
        x = jnp.where(j * tr + row < valid, x, 0.0)
    o_ref[...] = x.astype(o_ref.dtype)


def _cast_rows(wt, start, valid, tr, scaled_rows=None, scale=1.0):
    depth, _, k = wt.shape
    n = -(-valid // tr) * tr
    base = start // CAST_HALO * CAST_HALO
    assert base % tr == 0 and (start - base) % BF16_SUBLANES == 0
    in_specs = [pl.BlockSpec((None, tr, k), lambda l, j: (l, base // tr + j, 0))]
    if start != base:
        in_specs.append(pl.BlockSpec(
            (None, CAST_HALO, k), lambda l, j: (l, (base + tr * (j + 1)) // CAST_HALO, 0)))
    scaled_tiles = None
    if scaled_rows is not None:
        assert scaled_rows[0] % tr == 0 and scaled_rows[1] % tr == 0
        scaled_tiles = (scaled_rows[0] // tr, scaled_rows[1] // tr)
    return pl.pallas_call(
        functools.partial(_cast_rows_kernel, row_offset=start - base, valid=valid,
                          scaled_tiles=scaled_tiles, scale=scale),
        grid=(depth, n // tr),
        in_specs=in_specs,
        out_specs=pl.BlockSpec((None, tr, k), lambda l, j: (l, j, 0)),
        out_shape=jax.ShapeDtypeStruct((depth, n, k), BF16),
        compiler_params=_params(2),
        name="cast_rows",
    )(*([wt] * len(in_specs)))


_NT = (((1,), (1,)), ((), ()))


def _matmul_nt_kernel(a_ref, w_ref, o_ref):
    o_ref[...] = lax.dot_general(a_ref[...], w_ref[...], _NT,
                                 preferred_element_type=F32).astype(o_ref.dtype)


def _proj_idx(a, w_idx, layer, tm=1024):
    m, k = a.shape
    n = w_idx.shape[1]
    tm = min(tm, m)
    assert m % tm == 0
    return pl.pallas_call(
        _matmul_nt_kernel,
        grid=(m // tm,),
        in_specs=[pl.BlockSpec((tm, k), lambda i: (i, 0)),
                  pl.BlockSpec((None, n, k), lambda i: (layer, 0, 0))],
        out_specs=pl.BlockSpec((tm, n), lambda i: (i, 0)),
        out_shape=jax.ShapeDtypeStruct((m, n), F32),
        compiler_params=_params(1),
        name="proj_idx",
    )(a, w_idx)


def _proj_main_kernel(a_ref, wg_ref, wm_ref, o_ref, *, gate_tiles):
    j = pl.program_id(1)

    @pl.when(j < gate_tiles)
    def _():
        _matmul_nt_kernel(a_ref, wg_ref, o_ref)

    @pl.when(j >= gate_tiles)
    def _():
        _matmul_nt_kernel(a_ref, wm_ref, o_ref)


def _proj_main(a, w_gates, w_mix, layer, tm=1024, tn=1024):
    m, k = a.shape
    n_gates, n_mix = w_gates.shape[1], w_mix.shape[1]
    tm = min(tm, m)
    assert m % tm == 0 and n_gates % tn == 0 and n_mix % tn == 0
    gate_tiles, mix_tiles = n_gates // tn, n_mix // tn
    return pl.pallas_call(
        functools.partial(_proj_main_kernel, gate_tiles=gate_tiles),
        grid=(m // tm, gate_tiles + mix_tiles),
        in_specs=[pl.BlockSpec((tm, k), lambda i, j: (i, 0)),
                  pl.BlockSpec((None, tn, k), lambda i, j: (layer, jnp.minimum(j, gate_tiles - 1), 0)),
                  pl.BlockSpec((None, tn, k), lambda i, j: (layer, jnp.maximum(j - gate_tiles, 0), 0))],
        out_specs=pl.BlockSpec((tm, tn), lambda i, j: (i, j)),
        out_shape=jax.ShapeDtypeStruct((m, n_gates + n_mix), BF16),
        compiler_params=_params(2),
        name="proj_main",
    )(a, w_gates, w_mix)


def _mix_a_tile(u_ref, v_ref, z_ref, ws_ref, bt_ref):
    ti = lax.broadcasted_iota(I32, (CHUNK, CHUNK), 0)
    si = lax.broadcasted_iota(I32, (CHUNK, CHUNK), 1)
    tril = si <= ti
    gd = A_WIDTH // A_GROUPS
    groups = []
    for g in range(A_GROUPS):
        w = jnp.where(tril, ws_ref[g], 0.0).astype(BF16)
        b = bt_ref[:, g:g + 1]
        cols = slice(g * gd, (g + 1) * gd)
        chunks = []
        for n in range(u_ref.shape[0] // CHUNK):
            rows = slice(n * CHUNK, (n + 1) * CHUNK)
            mixed = jnp.dot(w, v_ref[rows, cols], preferred_element_type=F32) + b
            u = u_ref[rows, cols].astype(F32)
            z = z_ref[rows, cols].astype(F32)
            chunks.append(u * mixed * _silu(z))
        groups.append(jnp.concatenate(chunks, axis=0))
    return jnp.concatenate(groups, axis=1)


HALO = 16


def _mix_b_tile(bg_ref, cg_ref, x_ref, z_ref, cgh_ref, xh_ref, w_ref, first_of_sequence):
    cx = cg_ref[...].astype(F32) * x_ref[...].astype(F32)
    halo = cgh_ref[...].astype(F32) * xh_ref[...].astype(F32)
    halo = jnp.where(first_of_sequence, 0.0, halo)
    row = lax.broadcasted_iota(I32, cx.shape, 0)
    cx1 = pltpu.roll(cx, 1, axis=0)
    cx1 = jnp.where(row == 0, halo[HALO - 1:HALO, :], cx1)
    cx2 = pltpu.roll(cx, 2, axis=0)
    cx2 = jnp.where(row == 0, halo[HALO - 2:HALO - 1, :], cx2)
    cx2 = jnp.where(row == 1, halo[HALO - 1:HALO, :], cx2)
    conv = w_ref[0:1, :] * cx2 + w_ref[1:2, :] * cx1 + w_ref[2:3, :] * cx
    return bg_ref[...].astype(F32) * conv * _silu(z_ref[...].astype(F32))


def _t5_thresholds():
    max_exact = NUM_BUCKETS // 2
    def bucket(n):
        v = math.log(n / max_exact) / math.log(MAX_DISTANCE / max_exact) * (NUM_BUCKETS - max_exact)
        return max_exact + int(v)
    out = []
    for k in range(max_exact + 1, NUM_BUCKETS):
        out.append(next(n for n in range(max_exact, 4 * MAX_DISTANCE) if bucket(n) >= k))
    return tuple(out)


def _compare_exchange(x, i, j):
    x[i], x[j] = jnp.maximum(x[i], x[j]), jnp.minimum(x[i], x[j])


def _bitonic_merge(x):
    x, n = list(x), len(x)
    j = n // 2
    while j >= 1:
        for i in range(n):
            if i & j == 0:
                _compare_exchange(x, i, i | j)
        j //= 2
    return x


def _bitonic_sort(x):
    x, n = list(x), len(x)
    k = 2
    while k <= n:
        j = k // 2
        while j >= 1:
            for i in range(n):
                if i & j == 0:
                    lo, hi = (i, i | j) if i & k == 0 else (i | j, i)
                    _compare_exchange(x, lo, hi)
            j //= 2
        k *= 2
    return x


def _sortable(x):
    bits = pltpu.bitcast(x, I32)
    return bits ^ ((bits >> 31) & jnp.int32(0x7FFFFFFF))


def _dsa_kernel(relb_ref, iq_ref, iwq_ref, ik_ref, q_ref, k_ref, v_ref, cz_ref, o_ref,
                key_s, mb_s, wt_s, iqh_s, bias_s, m_s, l_s, acc_s, s_s, run_s, krun_s, dmax_s, thr_s, nge_s,
                *, tq, topk):
    qi = pl.program_id(1)
    n_kc = qi + 1
    kj = lax.broadcasted_iota(I32, (tq, tq), 0)
    qj = lax.broadcasted_iota(I32, (tq, tq), 1)
    log2e = math.log2(math.e)

    def causal(c):
        return (kj + (c - qi) * tq) <= qj

    @pl.when((pl.program_id(0) == 0) & (qi == 0))
    def _():
        thresholds = _t5_thresholds()
        for which in range(2):
            n = jnp.maximum(qj - kj + which * tq, 0)
            bucket = jnp.full((tq, tq), NUM_BUCKETS // 2, I32)
            for t in thresholds:
                bucket = bucket + (n >= t).astype(I32)
            bucket = jnp.where(n < NUM_BUCKETS // 2, n, bucket)
            for h in range(ATTN_HEADS):
                def fill(b, acc):
                    return jnp.where(bucket == b, relb_ref[b, h], acc)
                tile = lax.fori_loop(0, NUM_BUCKETS, fill, jnp.zeros((tq, tq), F32))
                bias_s[which, h] = (tile - relb_ref[NUM_BUCKETS - 1, h]) * log2e

    w_scale = IDX_HEADS ** -0.5 * IDX_HEAD_DIM ** -0.5
    wt_s[...] = (iwq_ref[...] * w_scale).T
    low_lanes = lax.broadcasted_iota(I32, (tq, V7X_LANES), 1) < IDX_HEAD_DIM
    heads_per_tile = V7X_LANES // IDX_HEAD_DIM
    for h in range(IDX_HEADS):
        x = iq_ref[:, (h // heads_per_tile) * V7X_LANES:(h // heads_per_tile + 1) * V7X_LANES]
        if h % heads_per_tile:
            x = pltpu.roll(x, V7X_LANES - (h % heads_per_tile) * IDX_HEAD_DIM, axis=1)
        iqh_s[h] = jnp.where(low_lanes, x, 0.0).astype(BF16)

    def score_tile(c):
        ks = pl.multiple_of(c * tq, tq)
        kc = ik_ref[pl.ds(ks, tq), :].astype(BF16)
        acc = jnp.zeros((tq, tq), F32)
        for h in range(IDX_HEADS):
            z = lax.dot_general(kc, iqh_s[h], (((1,), (1,)), ((), ())), preferred_element_type=F32)
            acc = acc + jnp.maximum(z, 0.0) * wt_s[IDX_HEAD_DIM + h:IDX_HEAD_DIM + h + 1, :]
        sc = jnp.where(causal(c), acc, NEG_INF)
        key_s[c] = _sortable(sc)
        for q_lanes in (slice(t * V7X_LANES, (t + 1) * V7X_LANES) for t in range(tq // V7X_LANES)):
            rows = _bitonic_sort([sc[SUBLANES * r:SUBLANES * (r + 1), q_lanes]
                                  for r in range(tq // SUBLANES)])
            run = [run_s[i, :, q_lanes] for i in range(RUN_DEPTH)]
            dropped = []
            for i, x in enumerate(reversed(rows)):
                j = RUN_DEPTH - len(rows) + i
                dropped.append(jnp.minimum(run[j], x))
                run[j] = jnp.maximum(run[j], x)
            run = _bitonic_merge(run)
            for i in range(RUN_DEPTH):
                run_s[i, :, q_lanes] = run[i]
            dmax_s[:, q_lanes] = functools.reduce(jnp.maximum, dropped, dmax_s[:, q_lanes])

    run_s[...] = jnp.full(run_s.shape, -jnp.inf, F32)
    dmax_s[...] = jnp.full(dmax_s.shape, -jnp.inf, F32)

    def score_tile_pair(i, carry):
        score_tile(2 * i)
        score_tile(2 * i + 1)
        return carry

    lax.fori_loop(0, n_kc // 2, score_tile_pair, 0)

    @pl.when(n_kc % 2 == 1)
    def _():
        score_tile(n_kc - 1)

    def radix_select(count_ge):
        def select_bit(i, thr):
            cand = thr + jnp.left_shift(jnp.int32(1), 31 - i)
            return jnp.where(count_ge(cand) >= topk, cand, thr)
        return lax.fori_loop(0, 32, select_bit, jnp.full((1, tq), INT_MIN, I32))

    krun_s[...] = _sortable(run_s[...])

    def count_ge_candidates(cand):
        ge = (krun_s[...] >= cand).astype(I32)
        return jnp.sum(jnp.sum(ge, axis=0), axis=0, keepdims=True)

    def count_ge_all(cand):
        def body(c, cnt):
            ge = (key_s[c] >= cand).astype(I32)
            return cnt + jnp.sum(ge.reshape(tq // SUBLANES, SUBLANES, tq), axis=0)
        cnt = lax.fori_loop(0, n_kc, body, jnp.zeros((SUBLANES, tq), I32))
        return jnp.sum(cnt, axis=0, keepdims=True)

    thr_s[...] = radix_select(count_ge_candidates)
    nge_s[...] = count_ge_candidates(thr_s[...])
    candidates_lossy = jnp.max((_sortable(dmax_s[...]) >= thr_s[...]).astype(I32)) > 0

    @pl.when(candidates_lossy)
    def _():
        thr_s[...] = radix_select(count_ge_all)
        nge_s[...] = count_ge_all(thr_s[...])

    thr = thr_s[...]

    def mask_tile(c, carry):
        mb_s[c] = jnp.where((key_s[c] >= thr) & causal(c), 0.0, NEG_INF)
        return carry

    lax.fori_loop(0, n_kc, mask_tile, 0)

    @pl.when(jnp.max(nge_s[...]) > topk)
    def _():
        need = (topk - count_ge_all(thr + 1)).astype(F32)
        prefix = (qj <= kj).astype(BF16)

        def mask_tile_ranked(c, seen):
            key = key_s[c]
            tied = key == thr
            tied_f = jnp.where(tied, 1.0, 0.0)
            rank = seen + jnp.dot(prefix, tied_f.astype(BF16), preferred_element_type=F32)
            keep = (key > thr) | (tied & (rank <= need))
            mb_s[c] = jnp.where(keep & causal(c), 0.0, NEG_INF)
            return seen + jnp.sum(tied_f, axis=0, keepdims=True)

        lax.fori_loop(0, n_kc, mask_tile_ranked, jnp.zeros((1, tq), F32))

    for h in range(ATTN_HEADS):
        m_s[h] = jnp.full((1, tq), NEG_INF, F32)
        l_s[h] = jnp.zeros((1, tq), F32)
        acc_s[h] = jnp.zeros((ATTN_HEAD_DIM, tq), F32)

    def attend(tiles):
        items = [(c, which, h) for c, which in tiles for h in range(ATTN_HEADS)]

        def logits(i):
            c, which, h = items[i]
            hs = slice(h * ATTN_HEAD_DIM, (h + 1) * ATTN_HEAD_DIM)
            kh = k_ref[pl.ds(pl.multiple_of(c * tq, tq), tq), hs]
            s = lax.dot_general(kh, q_ref[:, hs], (((1,), (1,)), ((), ())),
                                preferred_element_type=F32) + mb_s[c]
            if which is not None:
                s = s + bias_s[which, h]
            s_s[i % (HEADS_AHEAD + 1)] = s
            return jnp.max(s, axis=0, keepdims=True)

        def accumulate(i, m_cur):
            c, _, h = items[i]
            hs = slice(h * ATTN_HEAD_DIM, (h + 1) * ATTN_HEAD_DIM)
            vh = v_ref[pl.ds(pl.multiple_of(c * tq, tq), tq), hs]
            m_prev = m_s[h]
            m_new = jnp.maximum(m_prev, m_cur)
            alpha = jnp.exp2(m_prev - m_new)
            p = jnp.exp2(s_s[i % (HEADS_AHEAD + 1)] - m_new)
            l_s[h] = alpha * l_s[h] + jnp.sum(p, axis=0, keepdims=True)
            pv = lax.dot_general(vh, p.astype(BF16), (((0,), (0,)), ((), ())),
                                 preferred_element_type=F32)
            acc_s[h] = alpha * acc_s[h] + pv
            m_s[h] = m_new

        m_cur = [logits(i) for i in range(HEADS_AHEAD)]
        for i in range(len(items)):
            if i + HEADS_AHEAD < len(items):
                m_cur.append(logits(i + HEADS_AHEAD))
            accumulate(i, m_cur[i])

    n_far = jnp.maximum(qi - 1, 0)

    def attend_far_pair(i, carry):
        attend([(2 * i, None), (2 * i + 1, None)])
        return carry

    lax.fori_loop(0, n_far // 2, attend_far_pair, 0)

    @pl.when(n_far % 2 == 1)
    def _():
        attend([(n_far - 1, None)])

    @pl.when(qi >= 1)
    def _():
        attend([(qi - 1, 1), (qi, 0)])

    @pl.when(qi == 0)
    def _():
        attend([(qi, 0)])

    for h in range(ATTN_HEADS):
        hs = slice(h * ATTN_HEAD_DIM, (h + 1) * ATTN_HEAD_DIM)
        out = (acc_s[h] / l_s[h]).T
        o_ref[:, hs] = (out * _silu(cz_ref[:, hs].astype(F32))).astype(o_ref.dtype)


def _dsa(p_main, p_idx, rel_bias, sec0, batch, seq, topk, tq=256):
    m = p_main.shape[0]
    nq = seq // tq
    assert seq % tq == 0 and tq >= topk and tq // SUBLANES <= RUN_DEPTH
    assert RUN_DEPTH & (RUN_DEPTH - 1) == 0 and (tq // SUBLANES) & (tq // SUBLANES - 1) == 0
    idx_cols = IDX_HEADS * IDX_HEAD_DIM // V7X_LANES
    row = lambda b, q: b * nq + q
    kernel = functools.partial(_dsa_kernel, tq=tq, topk=topk)
    return pl.pallas_call(
        kernel,
        grid=(batch, nq),
        in_specs=[
            pl.BlockSpec(memory_space=pltpu.SMEM),
            pl.BlockSpec((tq, IDX_HEADS * IDX_HEAD_DIM), lambda b, q: (row(b, q), 0)),
            pl.BlockSpec((tq, V7X_LANES), lambda b, q: (row(b, q), idx_cols)),
            _resident((seq, V7X_LANES), lambda b, q: (b, idx_cols)),
            pl.BlockSpec((tq, SEC), lambda b, q: (row(b, q), sec0 + 7)),
            _resident((seq, SEC), lambda b, q: (b, sec0 + 8)),
            _resident((seq, SEC), lambda b, q: (b, sec0 + 9)),
            pl.BlockSpec((tq, SEC), lambda b, q: (row(b, q), sec0 + 10)),
        ],
        out_specs=pl.BlockSpec((tq, SEC), lambda b, q: (row(b, q), 0)),
        out_shape=jax.ShapeDtypeStruct((m, C_WIDTH), BF16),
        scratch_shapes=[
            pltpu.VMEM((nq, tq, tq), I32),
            pltpu.VMEM((nq, tq, tq), F32),
            pltpu.VMEM((V7X_LANES, tq), F32),
            pltpu.VMEM((IDX_HEADS, tq, V7X_LANES), BF16),
            pltpu.VMEM((2, ATTN_HEADS, tq, tq), F32),
            pltpu.VMEM((ATTN_HEADS, 1, tq), F32),
            pltpu.VMEM((ATTN_HEADS, 1, tq), F32),
            pltpu.VMEM((ATTN_HEADS, ATTN_HEAD_DIM, tq), F32),
            pltpu.VMEM((HEADS_AHEAD + 1, tq, tq), F32),
            pltpu.VMEM((RUN_DEPTH, SUBLANES, tq), F32),
            pltpu.VMEM((RUN_DEPTH, SUBLANES, tq), I32),
            pltpu.VMEM((SUBLANES, tq), F32),
            pltpu.VMEM((1, tq), I32),
            pltpu.VMEM((1, tq), I32),
        ],
        compiler_params=_params(2),
        name="dsa",
    )(rel_bias, p_idx, p_idx, p_idx, p_main, p_main, p_main, p_main)


def _merge_kernel(au_ref, av_ref, az_ref, ws_ref, bt_ref,
                  bg_ref, bc_ref, bx_ref, bz_ref, bch_ref, bxh_ref, cw_ref,
                  yc_ref, ga_ref, gb_ref, gc_ref, x_ref,
                  pa_ref, pb_ref, pc_ref, wo_ref, gn_ref, *out_refs, tiles_per_seq):
    y_a = _mix_a_tile(au_ref, av_ref, az_ref, ws_ref, bt_ref).astype(BF16)
    y_b = _mix_b_tile(bg_ref, bc_ref, bx_ref, bz_ref, bch_ref, bxh_ref, cw_ref,
                      pl.program_id(0) % tiles_per_seq == 0).astype(BF16)

    def branch(y, g_ref, p_ref):
        return _sigmoid(g_ref[...].astype(F32)) * jnp.dot(y, p_ref[...], preferred_element_type=F32)
    mix = branch(y_a, ga_ref, pa_ref) + branch(y_b, gb_ref, pb_ref) + branch(yc_ref[...], gc_ref, pc_ref)
    x_new = x_ref[...] + jnp.dot(mix.astype(BF16), wo_ref[...], preferred_element_type=F32)
    for o_ref in out_refs[:-1]:
        o_ref[...] = x_new
    out_refs[-1][...] = _rmsnorm_rows(x_new, gn_ref[...]).astype(out_refs[-1].dtype)


def _merge(y_c, p_main, x, ws, bias_t, conv_w, p_a, p_b, p_c, w_o, layer, sec0, seq,
           next_gain, normed_dtype, emit_x, tm=256):
    m, d = x.shape
    row_tile = pl.BlockSpec((tm, d), lambda i: (i, 0))
    out_specs = [row_tile] * emit_x + [row_tile]
    out_shape = [jax.ShapeDtypeStruct((m, d), F32)] * emit_x + [jax.ShapeDtypeStruct((m, d), normed_dtype)]
    assert tm % CHUNK == 0 and seq % tm == 0 and m % tm == 0
    col = lambda c: pl.BlockSpec((tm, SEC), lambda i, c=c: (i, sec0 + c))
    halo = lambda c: pl.BlockSpec(
        (HALO, SEC), lambda i, c=c: (jnp.maximum(i * (tm // HALO) - 1, 0), sec0 + c))
    gate = lambda c: pl.BlockSpec((tm, d), lambda i, c=c: (i, c))
    per_layer = lambda a: pl.BlockSpec((None,) + a.shape[1:], lambda i: (layer,) + (0,) * (a.ndim - 1))
    return pl.pallas_call(
        functools.partial(_merge_kernel, tiles_per_seq=seq // tm),
        grid=(m // tm,),
        in_specs=[col(0), col(1), col(2), per_layer(ws), per_layer(bias_t),
                  col(3), col(4), col(5), col(6), halo(4), halo(5), per_layer(conv_w),
                  pl.BlockSpec((tm, SEC), lambda i: (i, 0)), gate(0), gate(1), gate(2),
                  pl.BlockSpec((tm, d), lambda i: (i, 0)),
                  _resident((None, SEC, d), lambda i: (layer, 0, 0)),
                  _resident((None, SEC, d), lambda i: (layer, 0, 0)),
                  _resident((None, SEC, d), lambda i: (layer, 0, 0)),
                  _resident((None, d, d), lambda i: (layer, 0, 0)),
                  pl.BlockSpec((1, d), lambda i: (0, 0))],
        out_specs=out_specs,
        out_shape=out_shape,
        compiler_params=_params(1),
        name="merge",
    )(p_main, p_main, p_main, ws, bias_t, p_main, p_main, p_main, p_main, p_main, p_main, conv_w,
      y_c, p_main, p_main, p_main, x, p_a, p_b, p_c, w_o, next_gain.reshape(1, d))


def kernel(x, norm_g, w_in, a_ws, a_b, b_conv, p_a, p_b, p_c, w_o, rel_bias, final_g):
    batch, seq, d = x.shape
    depth = w_in.shape[0]
    m = batch * seq
    topk = min(TOPK_MAX, seq // 4)
    n_mix = 11 * SEC
    n_idx = IDX_HEADS * IDX_HEAD_DIM + IDX_HEAD_DIM + IDX_HEADS
    assert w_in.shape[2] == n_mix + n_idx + 3 * d and (3 * d) % SEC == 0
    sec0 = 3 * d // SEC

    q_scale = ATTN_HEAD_DIM ** -0.5 * math.log2(math.e)
    w_in_t = jnp.swapaxes(w_in, 1, 2)
    w_mix = _cast_rows(w_in_t, 0, n_mix, 512, scaled_rows=(7 * SEC, 8 * SEC), scale=q_scale)
    w_idx = _cast_rows(w_in_t, n_mix, n_idx, CAST_HALO)
    w_gates = _cast_rows(w_in_t, n_mix + n_idx, 3 * d, 512)
    a_bt = jnp.swapaxes(a_b, 1, 2)
    p_a16, p_b16, p_c16, w_o16 = (w.astype(BF16) for w in (p_a, p_b, p_c, w_o))

    xf = x.reshape(m, d)
    h = _rmsnorm(xf, norm_g[0], BF16)
    for l in range(depth):
        last = l == depth - 1
        p_main = _proj_main(h, w_gates, w_mix, l)
        p_idx = _proj_idx(h, w_idx, l)
        y_c = _dsa(p_main, p_idx, rel_bias, sec0, batch, seq, topk)
        outs = _merge(y_c, p_main, xf, a_ws, a_bt, b_conv, p_a16, p_b16, p_c16, w_o16, l, sec0, seq,
                      next_gain=final_g if last else norm_g[l + 1],
                      normed_dtype=x.dtype if last else BF16, emit_x=not last)
        xf, h = (None, outs[0]) if last else outs
    return h.reshape(batch, seq, d)
```

```python
import functools
import math

import jax
import jax.numpy as jnp
from jax import lax
from jax.experimental import pallas as pl
from jax.experimental.pallas import tpu as pltpu

EPS = 1e-6
CHUNK = 128
A_GROUPS = 8
A_WIDTH = 1024
B_WIDTH = 1024
SHORT_CONV = 3
ATTN_HEADS = 8
ATTN_HEAD_DIM = 128
C_WIDTH = ATTN_HEADS * ATTN_HEAD_DIM
IDX_HEADS = 16
IDX_HEAD_DIM = 64
TOPK_MAX = 256
NUM_BUCKETS = 32
MAX_DISTANCE = 128
NEG_INF = -1e30
SEC = 1024
RUN_DEPTH = 64
HEADS_AHEAD = 3

V7X_LANES = 128
SUBLANES = 8
BF16_SUBLANES = 16
CAST_HALO = 128
V7X_VMEM_BYTES = 64 * 1024 * 1024
VMEM_LIMIT = V7X_VMEM_BYTES - 8 * 1024 * 1024

F32 = jnp.float32
BF16 = jnp.bfloat16
I32 = jnp.int32
INT_MIN = -(2 ** 31)


def _params(n_axes):
    return pltpu.CompilerParams(
        dimension_semantics=("arbitrary",) * n_axes, vmem_limit_bytes=VMEM_LIMIT)


def _sigmoid(x):
    return 1.0 / (1.0 + jnp.exp(-x))


def _silu(x):
    return x * _sigmoid(x)


def _resident(block_shape, index_map):
    return pl.BlockSpec(block_shape, index_map, pipeline_mode=pl.Buffered(1))


def _rmsnorm_rows(x, g):
    return x * lax.rsqrt(jnp.mean(x * x, axis=-1, keepdims=True) + EPS) * g


def _rmsnorm_kernel(x_ref, g_ref, o_ref):
    o_ref[...] = _rmsnorm_rows(x_ref[...], g_ref[...]).astype(o_ref.dtype)


def _rmsnorm(x, g, out_dtype, tm=512):
    m, d = x.shape
    return pl.pallas_call(
        _rmsnorm_kernel,
        grid=(m // tm,),
        in_specs=[pl.BlockSpec((tm, d), lambda i: (i, 0)),
                  pl.BlockSpec((1, d), lambda i: (0, 0))],
        out_specs=pl.BlockSpec((tm, d), lambda i: (i, 0)),
        out_shape=jax.ShapeDtypeStruct((m, d), out_dtype),
        compiler_params=_params(1),
        name="rmsnorm",
    )(x, g.reshape(1, d))


def _cast_rows_kernel(*refs, row_offset, valid, scaled_tiles, scale):
    o_ref = refs[-1]
    tr = o_ref.shape[0]
    j = pl.program_id(1)
    x = refs[0][...]
    if row_offset:
        x = jnp.concatenate([x, refs[1][...]], axis=0)[row_offset:row_offset + tr, :]
    if scaled_tiles is not None:
        x = x * jnp.where((j >= scaled_tiles[0]) & (j < scaled_tiles[1]), scale, 1.0)
    if valid % tr:
        row = lax.broadcasted_iota(I32, x.shape, 0)
        x = jnp.where(j * tr + row < valid, x, 0.0)
    o_ref[...] = x.astype(o_ref.dtype)


def _cast_rows(wt, start, valid, tr, scaled_rows=None, scale=1.0):
    depth, _, k = wt.shape
    n = -(-valid // tr) * tr
    base = start // CAST_HALO * CAST_HALO
    assert base % tr == 0 and (start - base) % BF16_SUBLANES == 0
    in_specs = [pl.BlockSpec((None, tr, k), lambda l, j: (l, base // tr + j, 0))]
    if start != base:
        in_specs.append(pl.BlockSpec(
            (None, CAST_HALO, k), lambda l, j: (l, (base + tr * (j + 1)) // CAST_HALO, 0)))
    scaled_tiles = None
    if scaled_rows is not None:
        assert scaled_rows[0] % tr == 0 and scaled_rows[1] % tr == 0
        scaled_tiles = (scaled_rows[0] // tr, scaled_rows[1] // tr)
    return pl.pallas_call(
        functools.partial(_cast_rows_kernel, row_offset=start - base, valid=valid,
                          scaled_tiles=scaled_tiles, scale=scale),
        grid=(depth, n // tr),
        in_specs=in_specs,
        out_specs=pl.BlockSpec((None, tr, k), lambda l, j: (l, j, 0)),
        out_shape=jax.ShapeDtypeStruct((depth, n, k), BF16),
        compiler_params=_params(2),
        name="cast_rows",
    )(*([wt] * len(in_specs)))


_NT = (((1,), (1,)), ((), ()))


def _matmul_nt_kernel(a_ref, w_ref, o_ref):
    o_ref[...] = lax.dot_general(a_ref[...], w_ref[...], _NT,
                                 preferred_element_type=F32).astype(o_ref.dtype)


def _proj_idx(a, w_idx, layer, tm=1024):
    m, k = a.shape
    n = w_idx.shape[1]
    tm = min(tm, m)
    assert m % tm == 0
    return pl.pallas_call(
        _matmul_nt_kernel,
        grid=(m // tm,),
        in_specs=[pl.BlockSpec((tm, k), lambda i: (i, 0)),
                  pl.BlockSpec((None, n, k), lambda i: (layer, 0, 0))],
        out_specs=pl.BlockSpec((tm, n), lambda i: (i, 0)),
        out_shape=jax.ShapeDtypeStruct((m, n), F32),
        compiler_params=_params(1),
        name="proj_idx",
    )(a, w_idx)


def _proj_main_kernel(a_ref, wg_ref, wm_ref, o_ref, *, gate_tiles):
    j = pl.program_id(1)

    @pl.when(j < gate_tiles)
    def _():
        _matmul_nt_kernel(a_ref, wg_ref, o_ref)

    @pl.when(j >= gate_tiles)
    def _():
        _matmul_nt_kernel(a_ref, wm_ref, o_ref)


def _proj_main(a, w_gates, w_mix, layer, tm=2048, tn=1024):
    m, k = a.shape
    n_gates, n_mix = w_gates.shape[1], w_mix.shape[1]
    tm = min(tm, m)
    assert m % tm == 0 and n_gates % tn == 0 and n_mix % tn == 0
    gate_tiles, mix_tiles = n_gates // tn, n_mix // tn
    return pl.pallas_call(
        functools.partial(_proj_main_kernel, gate_tiles=gate_tiles),
        grid=(m // tm, gate_tiles + mix_tiles),
        in_specs=[pl.BlockSpec((tm, k), lambda i, j: (i, 0)),
                  pl.BlockSpec((None, tn, k), lambda i, j: (layer, jnp.minimum(j, gate_tiles - 1), 0)),
                  pl.BlockSpec((None, tn, k), lambda i, j: (layer, jnp.maximum(j - gate_tiles, 0), 0))],
        out_specs=pl.BlockSpec((tm, tn), lambda i, j: (i, j)),
        out_shape=jax.ShapeDtypeStruct((m, n_gates + n_mix), BF16),
        compiler_params=_params(2),
        name="proj_main",
    )(a, w_gates, w_mix)


def _mix_a_tile(u_ref, v_ref, z_ref, ws_ref, bt_ref):
    ti = lax.broadcasted_iota(I32, (CHUNK, CHUNK), 0)
    si = lax.broadcasted_iota(I32, (CHUNK, CHUNK), 1)
    tril = si <= ti
    gd = A_WIDTH // A_GROUPS
    groups = []
    for g in range(A_GROUPS):
        w = jnp.where(tril, ws_ref[g], 0.0).astype(BF16)
        b = bt_ref[:, g:g + 1]
        cols = slice(g * gd, (g + 1) * gd)
        chunks = []
        for n in range(u_ref.shape[0] // CHUNK):
            rows = slice(n * CHUNK, (n + 1) * CHUNK)
            mixed = jnp.dot(w, v_ref[rows, cols], preferred_element_type=F32) + b
            u = u_ref[rows, cols].astype(F32)
            z = z_ref[rows, cols].astype(F32)
            chunks.append(u * mixed * _silu(z))
        groups.append(jnp.concatenate(chunks, axis=0))
    return jnp.concatenate(groups, axis=1)


HALO = 16


def _mix_b_tile(bg_ref, cg_ref, x_ref, z_ref, cgh_ref, xh_ref, w_ref, first_of_sequence):
    cx = cg_ref[...].astype(F32) * x_ref[...].astype(F32)
    halo = cgh_ref[...].astype(F32) * xh_ref[...].astype(F32)
    halo = jnp.where(first_of_sequence, 0.0, halo)
    row = lax.broadcasted_iota(I32, cx.shape, 0)
    cx1 = pltpu.roll(cx, 1, axis=0)
    cx1 = jnp.where(row == 0, halo[HALO - 1:HALO, :], cx1)
    cx2 = pltpu.roll(cx, 2, axis=0)
    cx2 = jnp.where(row == 0, halo[HALO - 2:HALO - 1, :], cx2)
    cx2 = jnp.where(row == 1, halo[HALO - 1:HALO, :], cx2)
    conv = w_ref[0:1, :] * cx2 + w_ref[1:2, :] * cx1 + w_ref[2:3, :] * cx
    return bg_ref[...].astype(F32) * conv * _silu(z_ref[...].astype(F32))


def _t5_thresholds():
    max_exact = NUM_BUCKETS // 2
    def bucket(n):
        v = math.log(n / max_exact) / math.log(MAX_DISTANCE / max_exact) * (NUM_BUCKETS - max_exact)
        return max_exact + int(v)
    out = []
    for k in range(max_exact + 1, NUM_BUCKETS):
        out.append(next(n for n in range(max_exact, 4 * MAX_DISTANCE) if bucket(n) >= k))
    return tuple(out)


def _compare_exchange(x, i, j):
    x[i], x[j] = jnp.maximum(x[i], x[j]), jnp.minimum(x[i], x[j])


def _bitonic_merge(x):
    x, n = list(x), len(x)
    j = n // 2
    while j >= 1:
        for i in range(n):
            if i & j == 0:
                _compare_exchange(x, i, i | j)
        j //= 2
    return x


def _bitonic_sort(x):
    x, n = list(x), len(x)
    k = 2
    while k <= n:
        j = k // 2
        while j >= 1:
            for i in range(n):
                if i & j == 0:
                    lo, hi = (i, i | j) if i & k == 0 else (i | j, i)
                    _compare_exchange(x, lo, hi)
            j //= 2
        k *= 2
    return x


def _sortable(x):
    bits = pltpu.bitcast(x, I32)
    return bits ^ ((bits >> 31) & jnp.int32(0x7FFFFFFF))


def _dsa_kernel(relb_ref, iq_ref, iwq_ref, ik_ref, q_ref, k_ref, v_ref, cz_ref, o_ref,
                key_s, mb_s, wt_s, iqh_s, bias_s, m_s, l_s, acc_s, s_s, run_s, krun_s, dmax_s, thr_s, nge_s,
                *, tq, topk):
    qi = pl.program_id(1)
    n_kc = qi + 1
    kj = lax.broadcasted_iota(I32, (tq, tq), 0)
    qj = lax.broadcasted_iota(I32, (tq, tq), 1)
    log2e = math.log2(math.e)

    def causal(c):
        return (kj + (c - qi) * tq) <= qj

    @pl.when((pl.program_id(0) == 0) & (qi == 0))
    def _():
        thresholds = _t5_thresholds()
        for which in range(2):
            n = jnp.maximum(qj - kj + which * tq, 0)
            bucket = jnp.full((tq, tq), NUM_BUCKETS // 2, I32)
            for t in thresholds:
                bucket = bucket + (n >= t).astype(I32)
            bucket = jnp.where(n < NUM_BUCKETS // 2, n, bucket)
            for h in range(ATTN_HEADS):
                def fill(b, acc):
                    return jnp.where(bucket == b, relb_ref[b, h], acc)
                tile = lax.fori_loop(0, NUM_BUCKETS, fill, jnp.zeros((tq, tq), F32))
                bias_s[which, h] = (tile - relb_ref[NUM_BUCKETS - 1, h]) * log2e

    w_scale = IDX_HEADS ** -0.5 * IDX_HEAD_DIM ** -0.5
    wt_s[...] = (iwq_ref[...] * w_scale).T
    low_lanes = lax.broadcasted_iota(I32, (tq, V7X_LANES), 1) < IDX_HEAD_DIM
    heads_per_tile = V7X_LANES // IDX_HEAD_DIM
    for h in range(IDX_HEADS):
        x = iq_ref[:, (h // heads_per_tile) * V7X_LANES:(h // heads_per_tile + 1) * V7X_LANES]
        if h % heads_per_tile:
            x = pltpu.roll(x, V7X_LANES - (h % heads_per_tile) * IDX_HEAD_DIM, axis=1)
        iqh_s[h] = jnp.where(low_lanes, x, 0.0).astype(BF16)

    def score_tile(c):
        ks = pl.multiple_of(c * tq, tq)
        kc = ik_ref[pl.ds(ks, tq), :].astype(BF16)
        acc = jnp.zeros((tq, tq), F32)
        for h in range(IDX_HEADS):
            z = lax.dot_general(kc, iqh_s[h], (((1,), (1,)), ((), ())), preferred_element_type=F32)
            acc = acc + jnp.maximum(z, 0.0) * wt_s[IDX_HEAD_DIM + h:IDX_HEAD_DIM + h + 1, :]
        sc = jnp.where(causal(c), acc, NEG_INF)
        key_s[c] = _sortable(sc)
        for q_lanes in (slice(t * V7X_LANES, (t + 1) * V7X_LANES) for t in range(tq // V7X_LANES)):
            rows = _bitonic_sort([sc[SUBLANES * r:SUBLANES * (r + 1), q_lanes]
                                  for r in range(tq // SUBLANES)])
            run = [run_s[i, :, q_lanes] for i in range(RUN_DEPTH)]
            dropped = []
            for i, x in enumerate(reversed(rows)):
                j = RUN_DEPTH - len(rows) + i
                dropped.append(jnp.minimum(run[j], x))
                run[j] = jnp.maximum(run[j], x)
            run = _bitonic_merge(run)
            for i in range(RUN_DEPTH):
                run_s[i, :, q_lanes] = run[i]
            dmax_s[:, q_lanes] = functools.reduce(jnp.maximum, dropped, dmax_s[:, q_lanes])

    run_s[...] = jnp.full(run_s.shape, -jnp.inf, F32)
    dmax_s[...] = jnp.full(dmax_s.shape, -jnp.inf, F32)

    def score_tile_pair(i, carry):
        score_tile(2 * i)
        score_tile(2 * i + 1)
        return carry

    lax.fori_loop(0, n_kc // 2, score_tile_pair, 0)

    @pl.when(n_kc % 2 == 1)
    def _():
        score_tile(n_kc - 1)

    def radix_select(count_ge):
        def select_bit(i, thr):
            cand = thr + jnp.left_shift(jnp.int32(1), 31 - i)
            return jnp.where(count_ge(cand) >= topk, cand, thr)
        return lax.fori_loop(0, 32, select_bit, jnp.full((1, tq), INT_MIN, I32))

    krun_s[...] = _sortable(run_s[...])

    def count_ge_candidates(cand):
        ge = (krun_s[...] >= cand).astype(I32)
        return jnp.sum(jnp.sum(ge, axis=0), axis=0, keepdims=True)

    def count_ge_all(cand):
        def body(c, cnt):
            ge = (key_s[c] >= cand).astype(I32)
            return cnt + jnp.sum(ge.reshape(tq // SUBLANES, SUBLANES, tq), axis=0)
        cnt = lax.fori_loop(0, n_kc, body, jnp.zeros((SUBLANES, tq), I32))
        return jnp.sum(cnt, axis=0, keepdims=True)

    thr_s[...] = radix_select(count_ge_candidates)
    nge_s[...] = count_ge_candidates(thr_s[...])
    candidates_lossy = jnp.max((_sortable(dmax_s[...]) >= thr_s[...]).astype(I32)) > 0

    @pl.when(candidates_lossy)
    def _():
        thr_s[...] = radix_select(count_ge_all)
        nge_s[...] = count_ge_all(thr_s[...])

    thr = thr_s[...]

    def mask_tile(c, carry):
        mb_s[c] = jnp.where((key_s[c] >= thr) & causal(c), 0.0, NEG_INF)
        return carry

    lax.fori_loop(0, n_kc, mask_tile, 0)

    @pl.when(jnp.max(nge_s[...]) > topk)
    def _():
        need = (topk - count_ge_all(thr + 1)).astype(F32)
        prefix = (qj <= kj).astype(BF16)

        def mask_tile_ranked(c, seen):
            key = key_s[c]
            tied = key == thr
            tied_f = jnp.where(tied, 1.0, 0.0)
            rank = seen + jnp.dot(prefix, tied_f.astype(BF16), preferred_element_type=F32)
            keep = (key > thr) | (tied & (rank <= need))
            mb_s[c] = jnp.where(keep & causal(c), 0.0, NEG_INF)
            return seen + jnp.sum(tied_f, axis=0, keepdims=True)

        lax.fori_loop(0, n_kc, mask_tile_ranked, jnp.zeros((1, tq), F32))

    for h in range(ATTN_HEADS):
        m_s[h] = jnp.full((1, tq), NEG_INF, F32)
        l_s[h] = jnp.zeros((1, tq), F32)
        acc_s[h] = jnp.zeros((ATTN_HEAD_DIM, tq), F32)

    def attend(spans):
        items = [(c, n, which, h) for c, n, which in spans for h in range(ATTN_HEADS)]

        def logits(i):
            c, n, which, h = items[i]
            hs = slice(h * ATTN_HEAD_DIM, (h + 1) * ATTN_HEAD_DIM)
            kh = k_ref[pl.ds(pl.multiple_of(c * tq, tq), n * tq), hs]
            mask = jnp.concatenate([mb_s[c + t] for t in range(n)], axis=0)
            s = lax.dot_general(kh, q_ref[:, hs], (((1,), (1,)), ((), ())),
                                preferred_element_type=F32) + mask
            if which is not None:
                s = s + bias_s[which, h]
            s_s[i % (HEADS_AHEAD + 1), :n * tq] = s
            return jnp.max(s, axis=0, keepdims=True)

        def accumulate(i, m_cur):
            c, n, _, h = items[i]
            hs = slice(h * ATTN_HEAD_DIM, (h + 1) * ATTN_HEAD_DIM)
            vh = v_ref[pl.ds(pl.multiple_of(c * tq, tq), n * tq), hs]
            m_prev = m_s[h]
            m_new = jnp.maximum(m_prev, m_cur)
            alpha = jnp.exp2(m_prev - m_new)
            p = jnp.exp2(s_s[i % (HEADS_AHEAD + 1), :n * tq] - m_new)
            l_s[h] = alpha * l_s[h] + jnp.sum(p, axis=0, keepdims=True)
            pv = lax.dot_general(vh, p.astype(BF16), (((0,), (0,)), ((), ())),
                                 preferred_element_type=F32)
            acc_s[h] = alpha * acc_s[h] + pv
            m_s[h] = m_new

        m_cur = [logits(i) for i in range(HEADS_AHEAD)]
        for i in range(len(items)):
            if i + HEADS_AHEAD < len(items):
                m_cur.append(logits(i + HEADS_AHEAD))
            accumulate(i, m_cur[i])

    n_far = jnp.maximum(qi - 1, 0)

    def attend_far_pair(i, carry):
        attend([(2 * i, 2, None)])
        return carry

    lax.fori_loop(0, n_far // 2, attend_far_pair, 0)

    @pl.when(n_far % 2 == 1)
    def _():
        attend([(n_far - 1, 1, None)])

    @pl.when(qi >= 1)
    def _():
        attend([(qi - 1, 1, 1), (qi, 1, 0)])

    @pl.when(qi == 0)
    def _():
        attend([(qi, 1, 0)])

    for h in range(ATTN_HEADS):
        hs = slice(h * ATTN_HEAD_DIM, (h + 1) * ATTN_HEAD_DIM)
        out = (acc_s[h] / l_s[h]).T
        o_ref[:, hs] = (out * _silu(cz_ref[:, hs].astype(F32))).astype(o_ref.dtype)


def _dsa(p_main, p_idx, rel_bias, sec0, batch, seq, topk, tq=256):
    m = p_main.shape[0]
    nq = seq // tq
    assert seq % tq == 0 and tq >= topk and tq // SUBLANES <= RUN_DEPTH
    assert RUN_DEPTH & (RUN_DEPTH - 1) == 0 and (tq // SUBLANES) & (tq // SUBLANES - 1) == 0
    idx_cols = IDX_HEADS * IDX_HEAD_DIM // V7X_LANES
    row = lambda b, q: b * nq + q
    kernel = functools.partial(_dsa_kernel, tq=tq, topk=topk)
    return pl.pallas_call(
        kernel,
        grid=(batch, nq),
        in_specs=[
            pl.BlockSpec(memory_space=pltpu.SMEM),
            pl.BlockSpec((tq, IDX_HEADS * IDX_HEAD_DIM), lambda b, q: (row(b, q), 0)),
            pl.BlockSpec((tq, V7X_LANES), lambda b, q: (row(b, q), idx_cols)),
            _resident((seq, V7X_LANES), lambda b, q: (b, idx_cols)),
            pl.BlockSpec((tq, SEC), lambda b, q: (row(b, q), sec0 + 7)),
            _resident((seq, SEC), lambda b, q: (b, sec0 + 8)),
            _resident((seq, SEC), lambda b, q: (b, sec0 + 9)),
            pl.BlockSpec((tq, SEC), lambda b, q: (row(b, q), sec0 + 10)),
        ],
        out_specs=pl.BlockSpec((tq, SEC), lambda b, q: (row(b, q), 0)),
        out_shape=jax.ShapeDtypeStruct((m, C_WIDTH), BF16),
        scratch_shapes=[
            pltpu.VMEM((nq, tq, tq), I32),
            pltpu.VMEM((nq, tq, tq), F32),
            pltpu.VMEM((V7X_LANES, tq), F32),
            pltpu.VMEM((IDX_HEADS, tq, V7X_LANES), BF16),
            pltpu.VMEM((2, ATTN_HEADS, tq, tq), F32),
            pltpu.VMEM((ATTN_HEADS, 1, tq), F32),
            pltpu.VMEM((ATTN_HEADS, 1, tq), F32),
            pltpu.VMEM((ATTN_HEADS, ATTN_HEAD_DIM, tq), F32),
            pltpu.VMEM((HEADS_AHEAD + 1, 2 * tq, tq), F32),
            pltpu.VMEM((RUN_DEPTH, SUBLANES, tq), F32),
            pltpu.VMEM((RUN_DEPTH, SUBLANES, tq), I32),
            pltpu.VMEM((SUBLANES, tq), F32),
            pltpu.VMEM((1, tq), I32),
            pltpu.VMEM((1, tq), I32),
        ],
        compiler_params=_params(2),
        name="dsa",
    )(rel_bias, p_idx, p_idx, p_idx, p_main, p_main, p_main, p_main)


def _merge_kernel(au_ref, av_ref, az_ref, ws_ref, bt_ref,
                  bg_ref, bc_ref, bx_ref, bz_ref, bch_ref, bxh_ref, cw_ref,
                  yc_ref, ga_ref, gb_ref, gc_ref, x_ref,
                  pa_ref, pb_ref, pc_ref, wo_ref, gn_ref, *out_refs, tiles_per_seq):
    y_a = _mix_a_tile(au_ref, av_ref, az_ref, ws_ref, bt_ref).astype(BF16)
    y_b = _mix_b_tile(bg_ref, bc_ref, bx_ref, bz_ref, bch_ref, bxh_ref, cw_ref,
                      pl.program_id(0) % tiles_per_seq == 0).astype(BF16)

    def branch(y, g_ref, p_ref):
        return _sigmoid(g_ref[...].astype(F32)) * jnp.dot(y, p_ref[...], preferred_element_type=F32)
    mix = branch(y_a, ga_ref, pa_ref) + branch(y_b, gb_ref, pb_ref) + branch(yc_ref[...], gc_ref, pc_ref)
    x_new = x_ref[...] + jnp.dot(mix.astype(BF16), wo_ref[...], preferred_element_type=F32)
    for o_ref in out_refs[:-1]:
        o_ref[...] = x_new
    out_refs[-1][...] = _rmsnorm_rows(x_new, gn_ref[...]).astype(out_refs[-1].dtype)


def _merge(y_c, p_main, x, ws, bias_t, conv_w, p_a, p_b, p_c, w_o, layer, sec0, seq,
           next_gain, normed_dtype, emit_x, tm=256):
    m, d = x.shape
    row_tile = pl.BlockSpec((tm, d), lambda i: (i, 0))
    out_specs = [row_tile] * emit_x + [row_tile]
    out_shape = [jax.ShapeDtypeStruct((m, d), F32)] * emit_x + [jax.ShapeDtypeStruct((m, d), normed_dtype)]
    assert tm % CHUNK == 0 and seq % tm == 0 and m % tm == 0
    col = lambda c: pl.BlockSpec((tm, SEC), lambda i, c=c: (i, sec0 + c))
    halo = lambda c: pl.BlockSpec(
        (HALO, SEC), lambda i, c=c: (jnp.maximum(i * (tm // HALO) - 1, 0), sec0 + c))
    gate = lambda c: pl.BlockSpec((tm, d), lambda i, c=c: (i, c))
    per_layer = lambda a: pl.BlockSpec((None,) + a.shape[1:], lambda i: (layer,) + (0,) * (a.ndim - 1))
    return pl.pallas_call(
        functools.partial(_merge_kernel, tiles_per_seq=seq // tm),
        grid=(m // tm,),
        in_specs=[col(0), col(1), col(2), per_layer(ws), per_layer(bias_t),
                  col(3), col(4), col(5), col(6), halo(4), halo(5), per_layer(conv_w),
                  pl.BlockSpec((tm, SEC), lambda i: (i, 0)), gate(0), gate(1), gate(2),
                  pl.BlockSpec((tm, d), lambda i: (i, 0)),
                  _resident((None, SEC, d), lambda i: (layer, 0, 0)),
                  _resident((None, SEC, d), lambda i: (layer, 0, 0)),
                  _resident((None, SEC, d), lambda i: (layer, 0, 0)),
                  _resident((None, d, d), lambda i: (layer, 0, 0)),
                  pl.BlockSpec((1, d), lambda i: (0, 0))],
        out_specs=out_specs,
        out_shape=out_shape,
        compiler_params=_params(1),
        name="merge",
    )(p_main, p_main, p_main, ws, bias_t, p_main, p_main, p_main, p_main, p_main, p_main, conv_w,
      y_c, p_main, p_main, p_main, x, p_a, p_b, p_c, w_o, next_gain.reshape(1, d))


def kernel(x, norm_g, w_in, a_ws, a_b, b_conv, p_a, p_b, p_c, w_o, rel_bias, final_g):
    batch, seq, d = x.shape
    depth = w_in.shape[0]
    m = batch * seq
    topk = min(TOPK_MAX, seq // 4)
    n_mix = 11 * SEC
    n_idx = IDX_HEADS * IDX_HEAD_DIM + IDX_HEAD_DIM + IDX_HEADS
    assert w_in.shape[2] == n_mix + n_idx + 3 * d and (3 * d) % SEC == 0
    sec0 = 3 * d // SEC

    q_scale = ATTN_HEAD_DIM ** -0.5 * math.log2(math.e)
    w_in_t = jnp.swapaxes(w_in, 1, 2)
    w_mix = _cast_rows(w_in_t, 0, n_mix, 512, scaled_rows=(7 * SEC, 8 * SEC), scale=q_scale)
    w_idx = _cast_rows(w_in_t, n_mix, n_idx, CAST_HALO)
    w_gates = _cast_rows(w_in_t, n_mix + n_idx, 3 * d, 512)
    a_bt = jnp.swapaxes(a_b, 1, 2)
    p_a16, p_b16, p_c16, w_o16 = (w.astype(BF16) for w in (p_a, p_b, p_c, w_o))

    xf = x.reshape(m, d)
    h = _rmsnorm(xf, norm_g[0], BF16)
    for l in range(depth):
        last = l == depth - 1
        p_main = _proj_main(h, w_gates, w_mix, l)
        p_idx = _proj_idx(h, w_idx, l)
        y_c = _dsa(p_main, p_idx, rel_bias, sec0, batch, seq, topk)
        outs = _merge(y_c, p_main, xf, a_ws, a_bt, b_conv, p_a16, p_b16, p_c16, w_o16, l, sec0, seq,
                      next_gain=final_g if last else norm_g[l + 1],
                      normed_dtype=x.dtype if last else BF16, emit_x=not last)
        xf, h = (None, outs[0]) if last else outs
    return h.reshape(batch, seq, d)
```

```python
import functools
import math

import jax
import jax.numpy as jnp
from jax import lax
from jax.experimental import pallas as pl
from jax.experimental.pallas import tpu as pltpu

EPS = 1e-6
CHUNK = 128
A_GROUPS = 8
A_WIDTH = 1024
B_WIDTH = 1024
SHORT_CONV = 3
ATTN_HEADS = 8
ATTN_HEAD_DIM = 128
C_WIDTH = ATTN_HEADS * ATTN_HEAD_DIM
IDX_HEADS = 16
IDX_HEAD_DIM = 64
TOPK_MAX = 256
NUM_BUCKETS = 32
MAX_DISTANCE = 128
NEG_INF = -1e30
SEC = 1024
RUN_DEPTH = 64
HEADS_AHEAD = 3

V7X_LANES = 128
SUBLANES = 8
BF16_SUBLANES = 16
CAST_HALO = 128
V7X_VMEM_BYTES = 64 * 1024 * 1024
VMEM_LIMIT = V7X_VMEM_BYTES - 8 * 1024 * 1024

F32 = jnp.float32
BF16 = jnp.bfloat16
I32 = jnp.int32
INT_MIN = -(2 ** 31)


def _params(n_axes):
    return pltpu.CompilerParams(
        dimension_semantics=("arbitrary",) * n_axes, vmem_limit_bytes=VMEM_LIMIT)


def _sigmoid(x):
    return 1.0 / (1.0 + jnp.exp(-x))


def _silu(x):
    return x * _sigmoid(x)


def _resident(block_shape, index_map):
    return pl.BlockSpec(block_shape, index_map, pipeline_mode=pl.Buffered(1))


def _rmsnorm_rows(x, g):
    return x * lax.rsqrt(jnp.mean(x * x, axis=-1, keepdims=True) + EPS) * g


def _rmsnorm_kernel(x_ref, g_ref, o_ref):
    o_ref[...] = _rmsnorm_rows(x_ref[...], g_ref[...]).astype(o_ref.dtype)


def _rmsnorm(x, g, out_dtype, tm=512):
    m, d = x.shape
    return pl.pallas_call(
        _rmsnorm_kernel,
        grid=(m // tm,),
        in_specs=[pl.BlockSpec((tm, d), lambda i: (i, 0)),
                  pl.BlockSpec((1, d), lambda i: (0, 0))],
        out_specs=pl.BlockSpec((tm, d), lambda i: (i, 0)),
        out_shape=jax.ShapeDtypeStruct((m, d), out_dtype),
        compiler_params=_params(1),
        name="rmsnorm",
    )(x, g.reshape(1, d))


def _cast_rows_kernel(*refs, row_offset, valid, scaled_tiles, scale):
    o_ref = refs[-1]
    tr = o_ref.shape[0]
    j = pl.program_id(1)
    x = refs[0][...]
    if row_offset:
        x = jnp.concatenate([x, refs[1][...]], axis=0)[row_offset:row_offset + tr, :]
    if scaled_tiles is not None:
        x = x * jnp.where((j >= scaled_tiles[0]) & (j < scaled_tiles[1]), scale, 1.0)
    if valid % tr:
        row = lax.broadcasted_iota(I32, x.shape, 0)
        x = jnp.where(j * tr + row < valid, x, 0.0)
    o_ref[...] = x.astype(o_ref.dtype)


def _cast_rows(wt, start, valid, tr, scaled_rows=None, scale=1.0):
    depth, _, k = wt.shape
    n = -(-valid // tr) * tr
    base = start // CAST_HALO * CAST_HALO
    assert base % tr == 0 and (start - base) % BF16_SUBLANES == 0
    in_specs = [pl.BlockSpec((None, tr, k), lambda l, j: (l, base // tr + j, 0))]
    if start != base:
        in_specs.append(pl.BlockSpec(
            (None, CAST_HALO, k), lambda l, j: (l, (base + tr * (j + 1)) // CAST_HALO, 0)))
    scaled_tiles = None
    if scaled_rows is not None:
        assert scaled_rows[0] % tr == 0 and scaled_rows[1] % tr == 0
        scaled_tiles = (scaled_rows[0] // tr, scaled_rows[1] // tr)
    return pl.pallas_call(
        functools.partial(_cast_rows_kernel, row_offset=start - base, valid=valid,
                          scaled_tiles=scaled_tiles, scale=scale),
        grid=(depth, n // tr),
        in_specs=in_specs,
        out_specs=pl.BlockSpec((None, tr, k), lambda l, j: (l, j, 0)),
        out_shape=jax.ShapeDtypeStruct((depth, n, k), BF16),
        compiler_params=_params(2),
        name="cast_rows",
    )(*([wt] * len(in_specs)))


_NT = (((1,), (1,)), ((), ()))


def _matmul_nt_kernel(a_ref, w_ref, o_ref):
    o_ref[...] = lax.dot_general(a_ref[...], w_ref[...], _NT,
                                 preferred_element_type=F32).astype(o_ref.dtype)


def _proj_idx(a, w_idx, layer, tm=1024):
    m, k = a.shape
    n = w_idx.shape[1]
    tm = min(tm, m)
    assert m % tm == 0
    return pl.pallas_call(
        _matmul_nt_kernel,
        grid=(m // tm,),
        in_specs=[pl.BlockSpec((tm, k), lambda i: (i, 0)),
                  pl.BlockSpec((None, n, k), lambda i: (layer, 0, 0))],
        out_specs=pl.BlockSpec((tm, n), lambda i: (i, 0)),
        out_shape=jax.ShapeDtypeStruct((m, n), F32),
        compiler_params=_params(1),
        name="proj_idx",
    )(a, w_idx)


def _proj_main_kernel(a_ref, wg_ref, wm_ref, o_ref, *, gate_tiles):
    j = pl.program_id(1)

    @pl.when(j < gate_tiles)
    def _():
        _matmul_nt_kernel(a_ref, wg_ref, o_ref)

    @pl.when(j >= gate_tiles)
    def _():
        _matmul_nt_kernel(a_ref, wm_ref, o_ref)


def _proj_main(a, w_gates, w_mix, layer, tm=2048, tn=1024):
    m, k = a.shape
    n_gates, n_mix = w_gates.shape[1], w_mix.shape[1]
    tm = min(tm, m)
    assert m % tm == 0 and n_gates % tn == 0 and n_mix % tn == 0
    gate_tiles, mix_tiles = n_gates // tn, n_mix // tn
    return pl.pallas_call(
        functools.partial(_proj_main_kernel, gate_tiles=gate_tiles),
        grid=(m // tm, gate_tiles + mix_tiles),
        in_specs=[pl.BlockSpec((tm, k), lambda i, j: (i, 0)),
                  pl.BlockSpec((None, tn, k), lambda i, j: (layer, jnp.minimum(j, gate_tiles - 1), 0)),
                  pl.BlockSpec((None, tn, k), lambda i, j: (layer, jnp.maximum(j - gate_tiles, 0), 0))],
        out_specs=pl.BlockSpec((tm, tn), lambda i, j: (i, j)),
        out_shape=jax.ShapeDtypeStruct((m, n_gates + n_mix), BF16),
        compiler_params=_params(2),
        name="proj_main",
    )(a, w_gates, w_mix)


def _mix_a_tile(u_ref, v_ref, z_ref, ws_ref, bt_ref):
    ti = lax.broadcasted_iota(I32, (CHUNK, CHUNK), 0)
    si = lax.broadcasted_iota(I32, (CHUNK, CHUNK), 1)
    tril = si <= ti
    gd = A_WIDTH // A_GROUPS
    groups = []
    for g in range(A_GROUPS):
        w = jnp.where(tril, ws_ref[g], 0.0).astype(BF16)
        b = bt_ref[:, g:g + 1]
        cols = slice(g * gd, (g + 1) * gd)
        chunks = []
        for n in range(u_ref.shape[0] // CHUNK):
            rows = slice(n * CHUNK, (n + 1) * CHUNK)
            mixed = jnp.dot(w, v_ref[rows, cols], preferred_element_type=F32) + b
            u = u_ref[rows, cols].astype(F32)
            z = z_ref[rows, cols].astype(F32)
            chunks.append(u * mixed * _silu(z))
        groups.append(jnp.concatenate(chunks, axis=0))
    return jnp.concatenate(groups, axis=1)


HALO = 16


def _mix_b_tile(bg_ref, cg_ref, x_ref, z_ref, cgh_ref, xh_ref, w_ref, first_of_sequence):
    cx = cg_ref[...].astype(F32) * x_ref[...].astype(F32)
    halo = cgh_ref[...].astype(F32) * xh_ref[...].astype(F32)
    halo = jnp.where(first_of_sequence, 0.0, halo)
    row = lax.broadcasted_iota(I32, cx.shape, 0)
    cx1 = pltpu.roll(cx, 1, axis=0)
    cx1 = jnp.where(row == 0, halo[HALO - 1:HALO, :], cx1)
    cx2 = pltpu.roll(cx, 2, axis=0)
    cx2 = jnp.where(row == 0, halo[HALO - 2:HALO - 1, :], cx2)
    cx2 = jnp.where(row == 1, halo[HALO - 1:HALO, :], cx2)
    conv = w_ref[0:1, :] * cx2 + w_ref[1:2, :] * cx1 + w_ref[2:3, :] * cx
    return bg_ref[...].astype(F32) * conv * _silu(z_ref[...].astype(F32))


def _t5_thresholds():
    max_exact = NUM_BUCKETS // 2
    def bucket(n):
        v = math.log(n / max_exact) / math.log(MAX_DISTANCE / max_exact) * (NUM_BUCKETS - max_exact)
        return max_exact + int(v)
    out = []
    for k in range(max_exact + 1, NUM_BUCKETS):
        out.append(next(n for n in range(max_exact, 4 * MAX_DISTANCE) if bucket(n) >= k))
    return tuple(out)


def _compare_exchange(x, i, j):
    x[i], x[j] = jnp.maximum(x[i], x[j]), jnp.minimum(x[i], x[j])


def _bitonic_merge(x):
    x, n = list(x), len(x)
    j = n // 2
    while j >= 1:
        for i in range(n):
            if i & j == 0:
                _compare_exchange(x, i, i | j)
        j //= 2
    return x


def _bitonic_sort(x):
    x, n = list(x), len(x)
    k = 2
    while k <= n:
        j = k // 2
        while j >= 1:
            for i in range(n):
                if i & j == 0:
                    lo, hi = (i, i | j) if i & k == 0 else (i | j, i)
                    _compare_exchange(x, lo, hi)
            j //= 2
        k *= 2
    return x


def _sortable(x):
    bits = pltpu.bitcast(x, I32)
    return bits ^ ((bits >> 31) & jnp.int32(0x7FFFFFFF))


def _dsa_kernel(relb_ref, iq_ref, iwq_ref, ik_ref, q_ref, k_ref, v_ref, cz_ref, o_ref,
                key_s, mb_s, wt_s, iqh_s, bias_s, m_s, l_s, acc_s, s_s, run_s, krun_s, dmax_s, thr_s, nge_s,
                *, tq, topk):
    qi = pl.program_id(1)
    n_kc = qi + 1
    kj = lax.broadcasted_iota(I32, (tq, tq), 0)
    qj = lax.broadcasted_iota(I32, (tq, tq), 1)
    log2e = math.log2(math.e)

    def causal(c):
        return (kj + (c - qi) * tq) <= qj

    @pl.when((pl.program_id(0) == 0) & (qi == 0))
    def _():
        thresholds = _t5_thresholds()
        for which in range(2):
            n = jnp.maximum(qj - kj + which * tq, 0)
            bucket = jnp.full((tq, tq), NUM_BUCKETS // 2, I32)
            for t in thresholds:
                bucket = bucket + (n >= t).astype(I32)
            bucket = jnp.where(n < NUM_BUCKETS // 2, n, bucket)
            for h in range(ATTN_HEADS):
                def fill(b, acc):
                    return jnp.where(bucket == b, relb_ref[b, h], acc)
                tile = lax.fori_loop(0, NUM_BUCKETS, fill, jnp.zeros((tq, tq), F32))
                bias_s[which, h] = (tile - relb_ref[NUM_BUCKETS - 1, h]) * log2e

    w_scale = IDX_HEADS ** -0.5 * IDX_HEAD_DIM ** -0.5
    wt_s[...] = (iwq_ref[...] * w_scale).T
    low_lanes = lax.broadcasted_iota(I32, (tq, V7X_LANES), 1) < IDX_HEAD_DIM
    heads_per_tile = V7X_LANES // IDX_HEAD_DIM
    for h in range(IDX_HEADS):
        x = iq_ref[:, (h // heads_per_tile) * V7X_LANES:(h // heads_per_tile + 1) * V7X_LANES]
        if h % heads_per_tile:
            x = pltpu.roll(x, V7X_LANES - (h % heads_per_tile) * IDX_HEAD_DIM, axis=1)
        iqh_s[h] = jnp.where(low_lanes, x, 0.0).astype(BF16)

    def score_tile(c):
        ks = pl.multiple_of(c * tq, tq)
        kc = ik_ref[pl.ds(ks, tq), :].astype(BF16)
        acc = jnp.zeros((tq, tq), F32)
        for h in range(IDX_HEADS):
            z = lax.dot_general(kc, iqh_s[h], (((1,), (1,)), ((), ())), preferred_element_type=F32)
            acc = acc + jnp.maximum(z, 0.0) * wt_s[IDX_HEAD_DIM + h:IDX_HEAD_DIM + h + 1, :]
        sc = jnp.where(causal(c), acc, NEG_INF)
        key_s[c] = _sortable(sc)
        for q_lanes in (slice(t * V7X_LANES, (t + 1) * V7X_LANES) for t in range(tq // V7X_LANES)):
            rows = _bitonic_sort([sc[SUBLANES * r:SUBLANES * (r + 1), q_lanes]
                                  for r in range(tq // SUBLANES)])
            run = [run_s[i, :, q_lanes] for i in range(RUN_DEPTH)]
            dropped = []
            for i, x in enumerate(reversed(rows)):
                j = RUN_DEPTH - len(rows) + i
                dropped.append(jnp.minimum(run[j], x))
                run[j] = jnp.maximum(run[j], x)
            run = _bitonic_merge(run)
            for i in range(RUN_DEPTH):
                run_s[i, :, q_lanes] = run[i]
            dmax_s[:, q_lanes] = functools.reduce(jnp.maximum, dropped, dmax_s[:, q_lanes])

    run_s[...] = jnp.full(run_s.shape, -jnp.inf, F32)
    dmax_s[...] = jnp.full(dmax_s.shape, -jnp.inf, F32)

    def score_tile_pair(i, carry):
        score_tile(2 * i)
        score_tile(2 * i + 1)
        return carry

    lax.fori_loop(0, n_kc // 2, score_tile_pair, 0)

    @pl.when(n_kc % 2 == 1)
    def _():
        score_tile(n_kc - 1)

    def radix_select(count_ge, start=None, n_bits=32):
        def select_bit(i, thr):
            cand = thr + jnp.left_shift(jnp.int32(1), n_bits - 1 - i)
            return jnp.where(count_ge(cand) >= topk, cand, thr)
        start = jnp.full((1, tq), INT_MIN, I32) if start is None else start
        return lax.fori_loop(0, n_bits, select_bit, start)

    krun_s[...] = _sortable(run_s[...])

    def count_ge_candidates(cand):
        ge = (krun_s[...] >= cand).astype(I32)
        return jnp.sum(jnp.sum(ge, axis=0), axis=0, keepdims=True)

    def count_ge_all(cand):
        def body(c, cnt):
            ge = (key_s[c] >= cand).astype(I32)
            return cnt + jnp.sum(ge.reshape(tq // SUBLANES, SUBLANES, tq), axis=0)
        cnt = lax.fori_loop(0, n_kc, body, jnp.zeros((SUBLANES, tq), I32))
        return jnp.sum(cnt, axis=0, keepdims=True)

    pivot = krun_s[topk // SUBLANES - 1]
    low, high = jnp.min(pivot, axis=0, keepdims=True), jnp.max(pivot, axis=0, keepdims=True)
    n_bits = jnp.max(32 - lax.clz(low ^ high))
    shift = jnp.minimum(n_bits, 31)
    prefix = jnp.where(n_bits >= 32, INT_MIN, (low >> shift) << shift)
    thr_s[...] = radix_select(count_ge_candidates, prefix, n_bits)
    nge_s[...] = count_ge_candidates(thr_s[...])
    candidates_lossy = jnp.max((_sortable(dmax_s[...]) >= thr_s[...]).astype(I32)) > 0

    @pl.when(candidates_lossy)
    def _():
        thr_s[...] = radix_select(count_ge_all)
        nge_s[...] = count_ge_all(thr_s[...])

    thr = thr_s[...]

    def mask_tile(c, carry):
        mb_s[c] = jnp.where((key_s[c] >= thr) & causal(c), 0.0, NEG_INF)
        return carry

    lax.fori_loop(0, n_kc, mask_tile, 0)

    @pl.when(jnp.max(nge_s[...]) > topk)
    def _():
        need = (topk - count_ge_all(thr + 1)).astype(F32)
        prefix = (qj <= kj).astype(BF16)

        def mask_tile_ranked(c, seen):
            key = key_s[c]
            tied = key == thr
            tied_f = jnp.where(tied, 1.0, 0.0)
            rank = seen + jnp.dot(prefix, tied_f.astype(BF16), preferred_element_type=F32)
            keep = (key > thr) | (tied & (rank <= need))
            mb_s[c] = jnp.where(keep & causal(c), 0.0, NEG_INF)
            return seen + jnp.sum(tied_f, axis=0, keepdims=True)

        lax.fori_loop(0, n_kc, mask_tile_ranked, jnp.zeros((1, tq), F32))

    for h in range(ATTN_HEADS):
        m_s[h] = jnp.full((1, tq), NEG_INF, F32)
        l_s[h] = jnp.zeros((1, tq), F32)
        acc_s[h] = jnp.zeros((ATTN_HEAD_DIM, tq), F32)

    def attend(spans):
        items = [(c, n, which, h) for c, n, which in spans for h in range(ATTN_HEADS)]

        def logits(i):
            c, n, which, h = items[i]
            hs = slice(h * ATTN_HEAD_DIM, (h + 1) * ATTN_HEAD_DIM)
            kh = k_ref[pl.ds(pl.multiple_of(c * tq, tq), n * tq), hs]
            mask = jnp.concatenate([mb_s[c + t] for t in range(n)], axis=0)
            s = lax.dot_general(kh, q_ref[:, hs], (((1,), (1,)), ((), ())),
                                preferred_element_type=F32) + mask
            if which is not None:
                s = s + bias_s[which, h]
            s_s[i % (HEADS_AHEAD + 1), :n * tq] = s
            return jnp.max(s, axis=0, keepdims=True)

        def accumulate(i, m_cur):
            c, n, _, h = items[i]
            hs = slice(h * ATTN_HEAD_DIM, (h + 1) * ATTN_HEAD_DIM)
            vh = v_ref[pl.ds(pl.multiple_of(c * tq, tq), n * tq), hs]
            m_prev = m_s[h]
            m_new = jnp.maximum(m_prev, m_cur)
            alpha = jnp.exp2(m_prev - m_new)
            p = jnp.exp2(s_s[i % (HEADS_AHEAD + 1), :n * tq] - m_new)
            l_s[h] = alpha * l_s[h] + jnp.sum(p, axis=0, keepdims=True)
            pv = lax.dot_general(vh, p.astype(BF16), (((0,), (0,)), ((), ())),
                                 preferred_element_type=F32)
            acc_s[h] = alpha * acc_s[h] + pv
            m_s[h] = m_new

        m_cur = [logits(i) for i in range(HEADS_AHEAD)]
        for i in range(len(items)):
            if i + HEADS_AHEAD < len(items):
                m_cur.append(logits(i + HEADS_AHEAD))
            accumulate(i, m_cur[i])

    n_far = jnp.maximum(qi - 1, 0)

    def attend_far_pair(i, carry):
        attend([(2 * i, 2, None)])
        return carry

    lax.fori_loop(0, n_far // 2, attend_far_pair, 0)

    @pl.when(n_far % 2 == 1)
    def _():
        attend([(n_far - 1, 1, None)])

    @pl.when(qi >= 1)
    def _():
        attend([(qi - 1, 1, 1), (qi, 1, 0)])

    @pl.when(qi == 0)
    def _():
        attend([(qi, 1, 0)])

    for h in range(ATTN_HEADS):
        hs = slice(h * ATTN_HEAD_DIM, (h + 1) * ATTN_HEAD_DIM)
        out = (acc_s[h] / l_s[h]).T
        o_ref[:, hs] = (out * _silu(cz_ref[:, hs].astype(F32))).astype(o_ref.dtype)


def _dsa(p_main, p_idx, rel_bias, sec0, batch, seq, topk, tq=256):
    m = p_main.shape[0]
    nq = seq // tq
    assert seq % tq == 0 and tq >= topk and tq // SUBLANES <= RUN_DEPTH and topk % SUBLANES == 0
    assert RUN_DEPTH & (RUN_DEPTH - 1) == 0 and (tq // SUBLANES) & (tq // SUBLANES - 1) == 0
    idx_cols = IDX_HEADS * IDX_HEAD_DIM // V7X_LANES
    row = lambda b, q: b * nq + q
    kernel = functools.partial(_dsa_kernel, tq=tq, topk=topk)
    return pl.pallas_call(
        kernel,
        grid=(batch, nq),
        in_specs=[
            pl.BlockSpec(memory_space=pltpu.SMEM),
            pl.BlockSpec((tq, IDX_HEADS * IDX_HEAD_DIM), lambda b, q: (row(b, q), 0)),
            pl.BlockSpec((tq, V7X_LANES), lambda b, q: (row(b, q), idx_cols)),
            _resident((seq, V7X_LANES), lambda b, q: (b, idx_cols)),
            pl.BlockSpec((tq, SEC), lambda b, q: (row(b, q), sec0 + 7)),
            _resident((seq, SEC), lambda b, q: (b, sec0 + 8)),
            _resident((seq, SEC), lambda b, q: (b, sec0 + 9)),
            pl.BlockSpec((tq, SEC), lambda b, q: (row(b, q), sec0 + 10)),
        ],
        out_specs=pl.BlockSpec((tq, SEC), lambda b, q: (row(b, q), 0)),
        out_shape=jax.ShapeDtypeStruct((m, C_WIDTH), BF16),
        scratch_shapes=[
            pltpu.VMEM((nq, tq, tq), I32),
            pltpu.VMEM((nq, tq, tq), F32),
            pltpu.VMEM((V7X_LANES, tq), F32),
            pltpu.VMEM((IDX_HEADS, tq, V7X_LANES), BF16),
            pltpu.VMEM((2, ATTN_HEADS, tq, tq), F32),
            pltpu.VMEM((ATTN_HEADS, 1, tq), F32),
            pltpu.VMEM((ATTN_HEADS, 1, tq), F32),
            pltpu.VMEM((ATTN_HEADS, ATTN_HEAD_DIM, tq), F32),
            pltpu.VMEM((HEADS_AHEAD + 1, 2 * tq, tq), F32),
            pltpu.VMEM((RUN_DEPTH, SUBLANES, tq), F32),
            pltpu.VMEM((RUN_DEPTH, SUBLANES, tq), I32),
            pltpu.VMEM((SUBLANES, tq), F32),
            pltpu.VMEM((1, tq), I32),
            pltpu.VMEM((1, tq), I32),
        ],
        compiler_params=_params(2),
        name="dsa",
    )(rel_bias, p_idx, p_idx, p_idx, p_main, p_main, p_main, p_main)


def _merge_kernel(au_ref, av_ref, az_ref, ws_ref, bt_ref,
                  bg_ref, bc_ref, bx_ref, bz_ref, bch_ref, bxh_ref, cw_ref,
                  yc_ref, ga_ref, gb_ref, gc_ref, x_ref,
                  pa_ref, pb_ref, pc_ref, wo_ref, gn_ref, *out_refs, tiles_per_seq):
    y_a = _mix_a_tile(au_ref, av_ref, az_ref, ws_ref, bt_ref).astype(BF16)
    y_b = _mix_b_tile(bg_ref, bc_ref, bx_ref, bz_ref, bch_ref, bxh_ref, cw_ref,
                      pl.program_id(0) % tiles_per_seq == 0).astype(BF16)

    def branch(y, g_ref, p_ref):
        return _sigmoid(g_ref[...].astype(F32)) * jnp.dot(y, p_ref[...], preferred_element_type=F32)
    mix = branch(y_a, ga_ref, pa_ref) + branch(y_b, gb_ref, pb_ref) + branch(yc_ref[...], gc_ref, pc_ref)
    x_new = x_ref[...] + jnp.dot(mix.astype(BF16), wo_ref[...], preferred_element_type=F32)
    for o_ref in out_refs[:-1]:
        o_ref[...] = x_new
    out_refs[-1][...] = _rmsnorm_rows(x_new, gn_ref[...]).astype(out_refs[-1].dtype)


def _merge(y_c, p_main, x, ws, bias_t, conv_w, p_a, p_b, p_c, w_o, layer, sec0, seq,
           next_gain, normed_dtype, emit_x, tm=256):
    m, d = x.shape
    row_tile = pl.BlockSpec((tm, d), lambda i: (i, 0))
    out_specs = [row_tile] * emit_x + [row_tile]
    out_shape = [jax.ShapeDtypeStruct((m, d), F32)] * emit_x + [jax.ShapeDtypeStruct((m, d), normed_dtype)]
    assert tm % CHUNK == 0 and seq % tm == 0 and m % tm == 0
    col = lambda c: pl.BlockSpec((tm, SEC), lambda i, c=c: (i, sec0 + c))
    halo = lambda c: pl.BlockSpec(
        (HALO, SEC), lambda i, c=c: (jnp.maximum(i * (tm // HALO) - 1, 0), sec0 + c))
    gate = lambda c: pl.BlockSpec((tm, d), lambda i, c=c: (i, c))
    per_layer = lambda a: pl.BlockSpec((None,) + a.shape[1:], lambda i: (layer,) + (0,) * (a.ndim - 1))
    return pl.pallas_call(
        functools.partial(_merge_kernel, tiles_per_seq=seq // tm),
        grid=(m // tm,),
        in_specs=[col(0), col(1), col(2), per_layer(ws), per_layer(bias_t),
                  col(3), col(4), col(5), col(6), halo(4), halo(5), per_layer(conv_w),
                  pl.BlockSpec((tm, SEC), lambda i: (i, 0)), gate(0), gate(1), gate(2),
                  pl.BlockSpec((tm, d), lambda i: (i, 0)),
                  _resident((None, SEC, d), lambda i: (layer, 0, 0)),
                  _resident((None, SEC, d), lambda i: (layer, 0, 0)),
                  _resident((None, SEC, d), lambda i: (layer, 0, 0)),
                  _resident((None, d, d), lambda i: (layer, 0, 0)),
                  pl.BlockSpec((1, d), lambda i: (0, 0))],
        out_specs=out_specs,
        out_shape=out_shape,
        compiler_params=_params(1),
        name="merge",
    )(p_main, p_main, p_main, ws, bias_t, p_main, p_main, p_main, p_main, p_main, p_main, conv_w,
      y_c, p_main, p_main, p_main, x, p_a, p_b, p_c, w_o, next_gain.reshape(1, d))


def kernel(x, norm_g, w_in, a_ws, a_b, b_conv, p_a, p_b, p_c, w_o, rel_bias, final_g):
    batch, seq, d = x.shape
    depth = w_in.shape[0]
    m = batch * seq
    topk = min(TOPK_MAX, seq // 4)
    n_mix = 11 * SEC
    n_idx = IDX_HEADS * IDX_HEAD_DIM + IDX_HEAD_DIM + IDX_HEADS
    assert w_in.shape[2] == n_mix + n_idx + 3 * d and (3 * d) % SEC == 0
    sec0 = 3 * d // SEC

    q_scale = ATTN_HEAD_DIM ** -0.5 * math.log2(math.e)
    w_in_t = jnp.swapaxes(w_in, 1, 2)
    w_mix = _cast_rows(w_in_t, 0, n_mix, 512, scaled_rows=(7 * SEC, 8 * SEC), scale=q_scale)
    w_idx = _cast_rows(w_in_t, n_mix, n_idx, CAST_HALO)
    w_gates = _cast_rows(w_in_t, n_mix + n_idx, 3 * d, 512)
    a_bt = jnp.swapaxes(a_b, 1, 2)
    p_a16, p_b16, p_c16, w_o16 = (w.astype(BF16) for w in (p_a, p_b, p_c, w_o))

    xf = x.reshape(m, d)
    h = _rmsnorm(xf, norm_g[0], BF16)
    for l in range(depth):
        last = l == depth - 1
        p_main = _proj_main(h, w_gates, w_mix, l)
        p_idx = _proj_idx(h, w_idx, l)
        y_c = _dsa(p_main, p_idx, rel_bias, sec0, batch, seq, topk)
        outs = _merge(y_c, p_main, xf, a_ws, a_bt, b_conv, p_a16, p_b16, p_c16, w_o16, l, sec0, seq,
                      next_gain=final_g if last else norm_g[l + 1],
                      normed_dtype=x.dtype if last else BF16, emit_x=not last)
        xf, h = (None, outs[0]) if last else outs
    return h.reshape(batch, seq, d)
```
